```python
import jax, jax.numpy as jnp
from jax import lax
import numpy as np


D_MODEL = 1024
BATCH = 8
SEQ = 2048
DEPTH = 1
DEC_BATCH = 128
DEC_SEQ = 4
PAST_LEN = 16384
PAGE_SIZE = 128

N_RET_HEADS = 4
RET_HEAD_DIM = 256
D_RET = N_RET_HEADS * RET_HEAD_DIM
CHUNK = 128
D_CONV = D_MODEL
CONV_WIDTH = 31
N_MEM = 256
N_XA_HEADS = 4
XA_HEAD_DIM = 256
D_XA = N_XA_HEADS * XA_HEAD_DIM
N_BRANCH = 3
D_FF = 4 * D_MODEL
ROPE_BASE = 10000.0
EPS = 1e-6
D_IN = 4 * D_RET + 2 * D_CONV + D_XA + N_BRANCH * D_MODEL
SPLITS = (D_RET, 2 * D_RET, 3 * D_RET, 4 * D_RET, 4 * D_RET + 2 * D_CONV, 4 * D_RET + 2 * D_CONV + D_XA)

kernel_name = "gated_retention_conformer_memory_decoder_step"


def _rmsnorm(x, g):
    xf = x.astype(jnp.float32)
    y = xf * lax.rsqrt(jnp.mean(xf * xf, axis=-1, keepdims=True) + EPS) * g.astype(jnp.float32)
    return y.astype(x.dtype)


def _standardize(x):
    xf = x.astype(jnp.float32)
    mu = jnp.mean(xf, axis=-1, keepdims=True)
    xc = xf - mu
    return xc * lax.rsqrt(jnp.mean(xc * xc, axis=-1, keepdims=True) + EPS)


def _rope(x, pos):
    d = x.shape[-1]
    inv = ROPE_BASE ** (-jnp.arange(0, d, 2, dtype=jnp.float32) / d)
    ang = pos.astype(jnp.float32)[:, None] * inv[None, :]
    cos = jnp.cos(ang)[None, :, None, :]
    sin = jnp.sin(ang)[None, :, None, :]
    xf = x.astype(jnp.float32)
    x1, x2 = xf[..., : d // 2], xf[..., d // 2:]
    return jnp.concatenate([x1 * cos - x2 * sin, x2 * cos + x1 * sin], axis=-1)


def _log_gammas():
    return jnp.log1p(-jnp.exp2(-5.0 - jnp.arange(N_RET_HEADS, dtype=jnp.float32)))


def _retention(q, k, v, s0, log_gamma):
    B, H, L, _ = q.shape
    c = CHUNK if L % CHUNK == 0 else L
    n = L // c
    idx = jnp.arange(c, dtype=jnp.float32)
    diff = idx[:, None] - idx[None, :]
    causal = diff >= 0
    intra_decay = jnp.where(causal[None], jnp.exp(log_gamma[:, None, None] * jnp.where(causal, diff, 0.0)[None]), 0.0)
    q_decay = jnp.exp(log_gamma[:, None] * (idx + 1.0))[None, :, :, None]
    k_decay = jnp.exp(log_gamma[:, None] * (c - 1.0 - idx))[None, :, :, None]
    chunk_decay = jnp.exp(log_gamma * c)[None, :, None, None]

    def step(s, qkv):
        qc, kc, vc = qkv
        scores = jnp.einsum('bhid,bhjd->bhij', qc, kc) * intra_decay[None]
        o = jnp.einsum('bhij,bhjv->bhiv', scores, vc) + jnp.einsum('bhid,bhdv->bhiv', qc, s) * q_decay
        s = s * chunk_decay + jnp.einsum('bhjd,bhjv->bhdv', kc * k_decay, vc)
        return s, o

    def to_chunks(t):
        return jnp.moveaxis(t.reshape(B, H, n, c, t.shape[-1]), 2, 0)

    s_final, o = lax.scan(step, s0, (to_chunks(q), to_chunks(k), to_chunks(v)))
    o = jnp.moveaxis(o, 0, 2).reshape(B, H, L, v.shape[-1])
    return o, s_final


def _conformer_conv(glu_in, conv_buf, conv_w, conv_b, ln_g, ln_b, w_conv_out):
    a, gt = jnp.split(glu_in, 2, axis=-1)
    u = a * jax.nn.sigmoid(gt)
    ext = jnp.concatenate([conv_buf.astype(u.dtype), u], axis=1)
    y = lax.conv_general_dilated(ext, conv_w[:, None, :].astype(u.dtype), (1,), 'VALID',
                                 dimension_numbers=('NWC', 'WIO', 'NWC'),
                                 feature_group_count=D_CONV) + conv_b
    y = _standardize(y) * ln_g.astype(jnp.float32) + ln_b.astype(jnp.float32)
    y = jax.nn.silu(y).astype(glu_in.dtype) @ w_conv_out
    return y, ext[:, -(CONV_WIDTH - 1):]


def _cross_attention(q, mem_k, mem_v):
    s = jnp.einsum('blhd,bnhd->bhln', q.astype(jnp.float32), mem_k.astype(jnp.float32)) * (XA_HEAD_DIM ** -0.5)
    p = jax.nn.softmax(s, axis=-1)
    o = jnp.einsum('bhln,bnhd->blhd', p, mem_v.astype(jnp.float32))
    return o.astype(q.dtype)


def _memory_kv(mem, g_mem, w_mem_kv):
    B, N, _ = mem.shape
    kv = (_rmsnorm(mem, g_mem) @ w_mem_kv).reshape(B, N, 2, N_XA_HEADS, XA_HEAD_DIM)
    return kv[:, :, 0], kv[:, :, 1]


def _layer(x, pos, s_ret, conv_buf, mem_k, mem_v, g_mix, w_in, ret_gn_g, conv_w, conv_b,
           conv_ln_g, conv_ln_b, w_conv_out, w_out, g_ffn, w_up, w_down):
    B, L, _ = x.shape
    h = _rmsnorm(x, g_mix)
    proj = h @ w_in
    q, k, v, g, glu_in, q_xa, gate_logits = jnp.split(proj, SPLITS, axis=-1)

    qr = _rope(q.reshape(B, L, N_RET_HEADS, RET_HEAD_DIM), pos)
    kr = _rope(k.reshape(B, L, N_RET_HEADS, RET_HEAD_DIM), pos) * (RET_HEAD_DIM ** -0.5)
    vr = v.reshape(B, L, N_RET_HEADS, RET_HEAD_DIM).astype(jnp.float32)
    o, s_new = _retention(qr.transpose(0, 2, 1, 3), kr.transpose(0, 2, 1, 3), vr.transpose(0, 2, 1, 3),
                          s_ret.astype(jnp.float32), _log_gammas())
    o = _standardize(o.transpose(0, 2, 1, 3)).reshape(B, L, D_RET) * ret_gn_g.astype(jnp.float32)
    ret_out = (o * jax.nn.silu(g.astype(jnp.float32))).astype(x.dtype)

    conv_out, buf_new = _conformer_conv(glu_in, conv_buf, conv_w, conv_b, conv_ln_g, conv_ln_b, w_conv_out)

    xa_out = _cross_attention(q_xa.reshape(B, L, N_XA_HEADS, XA_HEAD_DIM), mem_k, mem_v).reshape(B, L, D_XA)

    gates = jax.nn.sigmoid(gate_logits).reshape(B, L, N_BRANCH, D_MODEL)
    merged = gates[:, :, 0] * ret_out + gates[:, :, 1] * conv_out + gates[:, :, 2] * xa_out
    x = x + merged @ w_out

    h2 = _rmsnorm(x, g_ffn)
    x = x + jnp.square(jax.nn.relu(h2 @ w_up)) @ w_down
    return x, s_new.astype(x.dtype), buf_new


def setup_inputs(seed: int = 0) -> dict:
    key = jax.random.key(seed)
    ks = jax.random.split(key, 24)

    def nrm(k, shape, scale):
        return jax.random.normal(k, shape, jnp.float32) * scale

    def gain(k, shape):
        return 1.0 + 0.02 * jax.random.normal(k, shape, jnp.float32)

    return {
        "x_prompt": nrm(ks[0], (BATCH, SEQ, D_MODEL), 1.0),
        "x_sample": nrm(ks[1], (DEC_BATCH, DEC_SEQ, D_MODEL), 1.0),
        "mem_prompt": nrm(ks[2], (BATCH, N_MEM, D_MODEL), 1.0),
        "state_ret": nrm(ks[3], (DEPTH, DEC_BATCH, N_RET_HEADS, RET_HEAD_DIM, RET_HEAD_DIM), 0.5),
        "state_conv": nrm(ks[4], (DEPTH, DEC_BATCH, CONV_WIDTH - 1, D_CONV), 0.5),
        "cache_mem_k": nrm(ks[5], (DEPTH, DEC_BATCH, N_MEM, N_XA_HEADS, XA_HEAD_DIM), 1.0),
        "cache_mem_v": nrm(ks[6], (DEPTH, DEC_BATCH, N_MEM, N_XA_HEADS, XA_HEAD_DIM), 1.0),
        "g_mix": gain(ks[7], (DEPTH, D_MODEL)),
        "w_in": nrm(ks[8], (DEPTH, D_MODEL, D_IN), D_MODEL ** -0.5),
        "ret_gn_g": gain(ks[9], (DEPTH, D_RET)),
        "conv_w": nrm(ks[10], (DEPTH, CONV_WIDTH, D_CONV), CONV_WIDTH ** -0.5),
        "conv_b": nrm(ks[11], (DEPTH, D_CONV), 0.02),
        "conv_ln_g": gain(ks[12], (DEPTH, D_CONV)),
        "conv_ln_b": nrm(ks[13], (DEPTH, D_CONV), 0.02),
        "w_conv_out": nrm(ks[14], (DEPTH, D_CONV, D_MODEL), D_CONV ** -0.5),
        "w_out": nrm(ks[15], (DEPTH, D_MODEL, D_MODEL), D_MODEL ** -0.5),
        "g_ffn": gain(ks[16], (DEPTH, D_MODEL)),
        "w_up": nrm(ks[17], (DEPTH, D_MODEL, D_FF), D_MODEL ** -0.5),
        "w_down": nrm(ks[18], (DEPTH, D_FF, D_MODEL), D_FF ** -0.5),
        "g_mem": gain(ks[19], (DEPTH, D_MODEL)),
        "w_mem_kv": nrm(ks[20], (DEPTH, D_MODEL, 2 * D_XA), D_MODEL ** -0.5),
        "g_final": gain(ks[21], (D_MODEL,)),
    }


def reference(x_prompt, x_sample, mem_prompt, state_ret, state_conv, cache_mem_k, cache_mem_v,
              g_mix, w_in, ret_gn_g, conv_w, conv_b, conv_ln_g, conv_ln_b, w_conv_out, w_out,
              g_ffn, w_up, w_down, g_mem, w_mem_kv, g_final):
    b_p, l_p = x_prompt.shape[0], x_prompt.shape[1]
    l_s = x_sample.shape[1]
    pos_p = jnp.arange(l_p)
    pos_s = PAST_LEN + jnp.arange(l_s)
    hp, hs = x_prompt, x_sample
    ret_p, conv_p, mk_p, mv_p, ret_s, conv_s = [], [], [], [], [], []
    for l in range(DEPTH):
        w = (g_mix[l], w_in[l], ret_gn_g[l], conv_w[l], conv_b[l], conv_ln_g[l], conv_ln_b[l],
             w_conv_out[l], w_out[l], g_ffn[l], w_up[l], w_down[l])
        mk, mv = _memory_kv(mem_prompt, g_mem[l], w_mem_kv[l])
        s0 = jnp.zeros((b_p, N_RET_HEADS, RET_HEAD_DIM, RET_HEAD_DIM), jnp.float32)
        buf0 = jnp.zeros((b_p, CONV_WIDTH - 1, D_CONV), x_prompt.dtype)
        hp, sp, cp = _layer(hp, pos_p, s0, buf0, mk, mv, *w)
        hs, ss, cs = _layer(hs, pos_s, state_ret[l], state_conv[l], cache_mem_k[l], cache_mem_v[l], *w)
        ret_p.append(sp)
        conv_p.append(cp)
        mk_p.append(mk)
        mv_p.append(mv)
        ret_s.append(ss)
        conv_s.append(cs)
    y_prompt = _rmsnorm(hp, g_final)
    y_sample = _rmsnorm(hs, g_final)
    return (y_prompt, y_sample, jnp.stack(ret_p), jnp.stack(conv_p), jnp.stack(mk_p), jnp.stack(mv_p),
            jnp.stack(ret_s), jnp.stack(conv_s))
```

```python
import functools

import jax
import jax.numpy as jnp
from jax import lax
from jax.experimental import pallas as pl
from jax.experimental.pallas import tpu as pltpu

F32 = jnp.float32
BF16 = jnp.bfloat16

D_MODEL = 1024
N_HEADS = 4
HEAD_DIM = 256
HALF = HEAD_DIM // 2
D_FF = 4 * D_MODEL
CONV_WIDTH = 31
HIST = CONV_WIDTH - 1
N_MEM = 256
PAST_LEN = 16384
ROPE_BASE = 10000.0
EPS = 1e-6
QK_SCALE = HEAD_DIM ** -0.5

OFF_Q, OFF_K, OFF_V, OFF_G = 0, 1024, 2048, 3072
OFF_GLU_A, OFF_GLU_G, OFF_XA, OFF_GATE = 4096, 5120, 6144, 7168
D_IN = 10240

SUBLANES = 8
HIST_PAD = 32
T_PROMPT = 256
T_FFN = 256
B_SAMPLE = 4
VMEM_LIMIT = 56 * 1024 * 1024

NT_DIMS = (((1,), (1,)), ((), ()))


def _dot(a, b):
    return jnp.dot(a, b, preferred_element_type=F32)


def _dot_nt(a, b):
    return lax.dot_general(a, b, NT_DIMS, preferred_element_type=F32)


def _rms(x, g):
    return x * lax.rsqrt(jnp.mean(x * x, axis=-1, keepdims=True) + EPS) * g


def _standardize(x):
    mu = jnp.mean(x, axis=-1, keepdims=True)
    xc = x - mu
    return xc * lax.rsqrt(jnp.mean(xc * xc, axis=-1, keepdims=True) + EPS)


def _silu(x):
    return x * jax.nn.sigmoid(x)


def _rope(x, cos, sin):
    x1, x2 = x[:, :HALF], x[:, HALF:]
    return jnp.concatenate([x1 * cos - x2 * sin, x2 * cos + x1 * sin], axis=-1)


def _const_spec(shape):
    return pl.BlockSpec(shape, lambda *_: (0,) * len(shape), pipeline_mode=pl.Buffered(1))


def _params(n_grid):
    return pltpu.CompilerParams(dimension_semantics=("arbitrary",) * n_grid,
                                vmem_limit_bytes=VMEM_LIMIT)


def _memkv_kernel(mem_ref, g_ref, w_ref, k_ref, v_ref, kb_ref, vb_ref):
    h = _rms(mem_ref[0], g_ref[...]).astype(BF16)
    kv = _dot(h, w_ref[...])
    k, v = kv[:, :D_MODEL], kv[:, D_MODEL:]
    k_ref[0] = k
    v_ref[0] = v
    kb_ref[0] = k.astype(BF16)
    vb_ref[0] = v.astype(BF16)


def _memkv(mem, g_mem, w_mem_kv):
    b = mem.shape[0]
    blk = pl.BlockSpec((1, N_MEM, D_MODEL), lambda i: (i, 0, 0))
    return pl.pallas_call(
        _memkv_kernel,
        grid=(b,),
        in_specs=[blk, _const_spec((1, D_MODEL)), _const_spec((D_MODEL, 2 * D_MODEL))],
        out_specs=[blk, blk, blk, blk],
        out_shape=[jax.ShapeDtypeStruct((b, N_MEM, D_MODEL), F32)] * 2
        + [jax.ShapeDtypeStruct((b, N_MEM, D_MODEL), BF16)] * 2,
        compiler_params=_params(1),
        name="memkv",
    )(mem, g_mem, w_mem_kv)


def _prompt_mixer_kernel(chunk_decay, n_chunks,
                         x_ref, cos_ref, sin_ref, gmix_ref, win_ref, gn_ref, cw_ref, cb_ref,
                         lng_ref, lnb_ref, wco_ref, kb_ref, vb_ref, dmat_ref, qdec_ref, kdec_ref,
                         merged_ref, sout_ref, cout_ref,
                         s_scr, ext_scr, y_scr, conv_scr):
    t = T_PROMPT
    c = pl.program_id(1)

    @pl.when(c == 0)
    def _():
        s_scr[...] = jnp.zeros_like(s_scr)
        ext_scr[0:HIST_PAD, :] = jnp.zeros((HIST_PAD, D_MODEL), F32)

    h = _rms(x_ref[0], gmix_ref[...]).astype(BF16)

    def proj(off):
        return _dot(h, win_ref[:, off:off + HEAD_DIM])

    for j in range(D_MODEL // HEAD_DIM):
        cols = slice(j * HEAD_DIM, (j + 1) * HEAD_DIM)
        ext_scr[HIST_PAD:HIST_PAD + t, cols] = (
            proj(OFF_GLU_A + j * HEAD_DIM) * jax.nn.sigmoid(proj(OFF_GLU_G + j * HEAD_DIM)))
    first = HIST_PAD - HIST
    for j in range(D_MODEL // 128):
        cols = slice(j * 128, (j + 1) * 128)
        acc = jnp.broadcast_to(cb_ref[:, cols], (t, 128))
        for w in range(CONV_WIDTH):
            acc = acc + ext_scr[first + w:first + w + t, cols] * cw_ref[w:w + 1, cols]
        y_scr[:, cols] = acc
    act = _silu(_standardize(y_scr[...]) * lng_ref[...] + lnb_ref[...])
    conv_scr[...] = _dot(act.astype(BF16), wco_ref[...])

    @pl.when(c == n_chunks - 1)
    def _():
        cout_ref[0] = ext_scr[t + HIST_PAD - HIST:t + HIST_PAD, :]

    ext_scr[0:HIST_PAD, :] = ext_scr[t:t + HIST_PAD, :]

    cos, sin = cos_ref[...], sin_ref[...]
    for hd in range(N_HEADS):
        sl = slice(hd * HEAD_DIM, (hd + 1) * HEAD_DIM)
        qb = _rope(proj(OFF_Q + hd * HEAD_DIM), cos, sin).astype(BF16)
        kr = _rope(proj(OFF_K + hd * HEAD_DIM), cos, sin) * QK_SCALE
        vb = proj(OFF_V + hd * HEAD_DIM).astype(BF16)
        scores = _dot_nt(qb, kr.astype(BF16)) * dmat_ref[hd]
        s_old = s_scr[hd]
        o = _dot(scores.astype(BF16), vb) + _dot(qb, s_old.astype(BF16)) * qdec_ref[hd]
        kd_t = (kr * kdec_ref[hd]).T.astype(BF16)
        s_scr[hd] = s_old * chunk_decay[hd] + _dot(kd_t, vb)
        ret = _standardize(o) * gn_ref[:, sl] * _silu(proj(OFF_G + hd * HEAD_DIM))
        qx = proj(OFF_XA + hd * HEAD_DIM).astype(BF16)
        s = _dot_nt(qx, kb_ref[0, :, sl]) * QK_SCALE
        e = jnp.exp(s - jnp.max(s, axis=-1, keepdims=True))
        p = e / jnp.sum(e, axis=-1, keepdims=True)
        xa = _dot(p.astype(BF16), vb_ref[0, :, sl])
        g0 = jax.nn.sigmoid(proj(OFF_GATE + hd * HEAD_DIM))
        g1 = jax.nn.sigmoid(proj(OFF_GATE + D_MODEL + hd * HEAD_DIM))
        g2 = jax.nn.sigmoid(proj(OFF_GATE + 2 * D_MODEL + hd * HEAD_DIM))
        merged_ref[0, :, sl] = (g0 * ret + g1 * conv_scr[:, sl] + g2 * xa).astype(BF16)

    @pl.when(c == n_chunks - 1)
    def _():
        sout_ref[0] = s_scr[...]


def _decay_tables(log_gamma, c):
    idx = jnp.arange(c, dtype=F32)
    diff = idx[:, None] - idx[None, :]
    causal = diff >= 0
    intra = jnp.where(causal[None],
                      jnp.exp(log_gamma[:, None, None] * jnp.where(causal, diff, 0.0)[None]), 0.0)
    q_decay = jnp.exp(log_gamma[:, None] * (idx + 1.0))
    k_decay = jnp.exp(log_gamma[:, None] * (c - 1.0 - idx))
    chunk_decay = jnp.exp(log_gamma * c)
    return intra, q_decay, k_decay, chunk_decay


def _log_gammas():
    return jnp.log1p(-jnp.exp2(-5.0 - jnp.arange(N_HEADS, dtype=F32)))


def _rope_tables(pos):
    inv = ROPE_BASE ** (-jnp.arange(0, HEAD_DIM, 2, dtype=F32) / HEAD_DIM)
    ang = pos.astype(F32)[:, None] * inv[None, :]
    return jnp.cos(ang), jnp.sin(ang)


def _prompt_mixer(x, kb, vb, w):
    b, l, _ = x.shape
    t = T_PROMPT
    n_chunks = l // t
    cos, sin = _rope_tables(jnp.arange(l))
    intra, q_decay, k_decay, chunk_decay = _decay_tables(_log_gammas(), t)
    qdec = jnp.broadcast_to(q_decay[:, :, None], (N_HEADS, t, HEAD_DIM))
    kdec = jnp.broadcast_to(k_decay[:, :, None], (N_HEADS, t, HEAD_DIM))

    def body(cd_ref, *refs):
        cd = tuple(cd_ref[i] for i in range(N_HEADS))
        _prompt_mixer_kernel(cd, n_chunks, *refs)

    tok = pl.BlockSpec((1, t, D_MODEL), lambda i, j: (i, j, 0))
    rope = pl.BlockSpec((t, HALF), lambda i, j: (j, 0))
    mem = pl.BlockSpec((1, N_MEM, D_MODEL), lambda i, j: (i, 0, 0))
    row = _const_spec((1, D_MODEL))
    table = _const_spec((N_HEADS, t, HEAD_DIM))
    return pl.pallas_call(
        body,
        grid=(b, n_chunks),
        in_specs=[pl.BlockSpec(memory_space=pltpu.SMEM),
                  tok, rope, rope, row, _const_spec((D_MODEL, D_IN)), row,
                  _const_spec((CONV_WIDTH, D_MODEL)), row, row, row,
                  _const_spec((D_MODEL, D_MODEL)), mem, mem, table, table, table],
        out_specs=[tok,
                   pl.BlockSpec((1, N_HEADS, HEAD_DIM, HEAD_DIM), lambda i, j: (i, 0, 0, 0)),
                   pl.BlockSpec((1, HIST, D_MODEL), lambda i, j: (i, 0, 0))],
        out_shape=[jax.ShapeDtypeStruct((b, l, D_MODEL), BF16),
                   jax.ShapeDtypeStruct((b, N_HEADS, HEAD_DIM, HEAD_DIM), F32),
                   jax.ShapeDtypeStruct((b, HIST, D_MODEL), F32)],
        scratch_shapes=[pltpu.VMEM((N_HEADS, HEAD_DIM, HEAD_DIM), F32),
                        pltpu.VMEM((HIST_PAD + t, D_MODEL), F32),
                        pltpu.VMEM((t, D_MODEL), F32),
                        pltpu.VMEM((t, D_MODEL), F32)],
        compiler_params=_params(2),
        name="prompt_mixer",
    )(chunk_decay, x, cos, sin, w["g_mix"], w["w_in"], w["ret_gn_g"], w["conv_w"],
      w["conv_b"], w["conv_ln_g"], w["conv_ln_b"], w["w_conv_out"], kb, vb, intra, qdec, kdec)


def _ffn_kernel(x_ref, m_ref, wout_ref, gffn_ref, wup_ref, wdown_ref, gfin_ref, y_ref):
    x1 = x_ref[...] + _dot(m_ref[...].astype(BF16), wout_ref[...])
    h2 = _rms(x1, gffn_ref[...]).astype(BF16)
    a = jnp.square(jnp.maximum(_dot(h2, wup_ref[...]), 0.0)).astype(BF16)
    x2 = x1 + _dot(a, wdown_ref[...])
    y_ref[...] = _rms(x2, gfin_ref[...])


def _ffn(x, merged, w):
    m = x.shape[0]
    tok = pl.BlockSpec((T_FFN, D_MODEL), lambda i: (i, 0))
    row = _const_spec((1, D_MODEL))
    return pl.pallas_call(
        _ffn_kernel,
        grid=(m // T_FFN,),
        in_specs=[tok, tok, _const_spec((D_MODEL, D_MODEL)), row, _const_spec((D_MODEL, D_FF)),
                  _const_spec((D_FF, D_MODEL)), row],
        out_specs=tok,
        out_shape=jax.ShapeDtypeStruct((m, D_MODEL), F32),
        compiler_params=_params(1),
        name="ffn",
    )(x, merged, w["w_out"], w["g_ffn"], w["w_up"], w["w_down"], w["g_final"])


SAMPLE_PROJ_COLS = 1024


def _sample_proj_kernel(x_ref, g_ref, w_ref, o_ref):
    h = _rms(x_ref[...], g_ref[...]).astype(BF16)
    o_ref[...] = _dot(h, w_ref[...])


def _sample_proj(x, w):
    m = x.shape[0]
    return pl.pallas_call(
        _sample_proj_kernel,
        grid=(D_IN // SAMPLE_PROJ_COLS,),
        in_specs=[_const_spec((m, D_MODEL)), _const_spec((1, D_MODEL)),
                  pl.BlockSpec((D_MODEL, SAMPLE_PROJ_COLS), lambda j: (0, j))],
        out_specs=pl.BlockSpec((m, SAMPLE_PROJ_COLS), lambda j: (0, j)),
        out_shape=jax.ShapeDtypeStruct((m, D_IN), F32),
        compiler_params=_params(1),
        name="sample_proj",
    )(x, w["g_mix"], w["w_in"])


def _sample_mixer_kernel(chunk_decay, l,
                         p_ref, cos_ref, sin_ref, s_ref, cbuf_ref, k_ref, v_ref,
                         gn_ref, cw_ref, cb_ref, lng_ref, lnb_ref, wco_ref,
                         dmat_ref, qdec_ref, kdec_ref,
                         merged_ref, sout_ref, cout_ref,
                         ext_scr, y_scr):
    rows = B_SAMPLE * l
    pad = 128
    row_id = lax.broadcasted_iota(jnp.int32, (rows, HEAD_DIM), 0)
    col_id = lax.broadcasted_iota(jnp.int32, (HEAD_DIM, pad), 1)
    row_of = [(row_id >= bi * l) & (row_id < (bi + 1) * l) for bi in range(B_SAMPLE)]
    col_of = [(col_id >= bi * l) & (col_id < (bi + 1) * l) for bi in range(B_SAMPLE)]

    def proj(off, n=HEAD_DIM):
        return p_ref[:, off:off + n]

    u = proj(OFF_GLU_A, D_MODEL) * jax.nn.sigmoid(proj(OFF_GLU_G, D_MODEL))
    for bi in range(B_SAMPLE):
        ext_scr[0:HIST, :] = cbuf_ref[bi]
        ext_scr[HIST:HIST + l, :] = u[bi * l:(bi + 1) * l, :]
        cout_ref[bi] = ext_scr[l:l + HIST, :]
        for i in range(l):
            y_scr[bi * l + i:bi * l + i + 1, :] = (
                jnp.sum(ext_scr[i:i + CONV_WIDTH, :] * cw_ref[...], axis=0, keepdims=True)
                + cb_ref[...])
    act = _silu(_standardize(y_scr[...]) * lng_ref[...] + lnb_ref[...])
    conv = _dot(act.astype(BF16), wco_ref[...])

    cos, sin = cos_ref[...], sin_ref[...]
    zeros_pad = jnp.zeros((pad - rows, HEAD_DIM), F32)
    for hd in range(N_HEADS):
        sl = slice(hd * HEAD_DIM, (hd + 1) * HEAD_DIM)
        qb = _rope(proj(OFF_Q + hd * HEAD_DIM), cos, sin).astype(BF16)
        kr = _rope(proj(OFF_K + hd * HEAD_DIM), cos, sin) * QK_SCALE
        v = proj(OFF_V + hd * HEAD_DIM)
        kb_pad = jnp.concatenate([kr, zeros_pad], axis=0).astype(BF16)
        vb_pad = jnp.concatenate([v, zeros_pad], axis=0).astype(BF16)
        scores = _dot_nt(qb, kb_pad) * dmat_ref[hd]
        o = _dot(scores.astype(BF16), vb_pad)
        kd_t = jnp.concatenate([kr * kdec_ref[hd], zeros_pad], axis=0).T
        o_cross = jnp.zeros((rows, HEAD_DIM), F32)
        for bi in range(B_SAMPLE):
            s_old = s_ref[bi, hd]
            o_cross = jnp.where(row_of[bi], _dot(qb, s_old.astype(BF16)), o_cross)
            kd_bi = jnp.where(col_of[bi], kd_t, 0.0).astype(BF16)
            sout_ref[bi, hd] = s_old * chunk_decay[hd] + _dot(kd_bi, vb_pad)
        o = o + o_cross * qdec_ref[hd]
        ret = _standardize(o) * gn_ref[:, sl] * _silu(proj(OFF_G + hd * HEAD_DIM))
        qx = proj(OFF_XA + hd * HEAD_DIM).astype(BF16)
        s = jnp.zeros((rows, N_MEM), F32)
        for bi in range(B_SAMPLE):
            s = jnp.where(row_of[bi], _dot_nt(qx, k_ref[bi, :, sl].astype(BF16)), s)
        s = s * QK_SCALE
        e = jnp.exp(s - jnp.max(s, axis=-1, keepdims=True))
        pb = (e / jnp.sum(e, axis=-1, keepdims=True)).astype(BF16)
        xa = jnp.zeros((rows, HEAD_DIM), F32)
        for bi in range(B_SAMPLE):
            xa = jnp.where(row_of[bi], _dot(pb, v_ref[bi, :, sl].astype(BF16)), xa)
        g0 = jax.nn.sigmoid(proj(OFF_GATE + hd * HEAD_DIM))
        g1 = jax.nn.sigmoid(proj(OFF_GATE + D_MODEL + hd * HEAD_DIM))
        g2 = jax.nn.sigmoid(proj(OFF_GATE + 2 * D_MODEL + hd * HEAD_DIM))
        merged_ref[:, sl] = g0 * ret + g1 * conv[:, sl] + g2 * xa


def _sample_mixer(proj, state_ret, state_conv, cache_k, cache_v, w, l):
    nb = state_ret.shape[0]
    rows = B_SAMPLE * l
    pad = 128
    cos, sin = _rope_tables(PAST_LEN + jnp.arange(l))
    cos, sin = jnp.tile(cos, (B_SAMPLE, 1)), jnp.tile(sin, (B_SAMPLE, 1))
    intra, q_decay, k_decay, chunk_decay = _decay_tables(_log_gammas(), l)
    same_req = (jnp.arange(rows)[:, None] // l) == (jnp.arange(pad)[None, :] // l)
    dmat = jnp.where(same_req[None], jnp.pad(jnp.tile(intra, (1, B_SAMPLE, B_SAMPLE)),
                                              ((0, 0), (0, 0), (0, pad - rows))), 0.0)
    qdec = jnp.broadcast_to(jnp.tile(q_decay, (1, B_SAMPLE))[:, :, None], (N_HEADS, rows, HEAD_DIM))
    kdec = jnp.broadcast_to(jnp.tile(k_decay, (1, B_SAMPLE))[:, :, None], (N_HEADS, rows, HEAD_DIM))

    def body(cd_ref, *refs):
        cd = tuple(cd_ref[i] for i in range(N_HEADS))
        _sample_mixer_kernel(cd, l, *refs)

    row = _const_spec((1, D_MODEL))
    state = pl.BlockSpec((B_SAMPLE, N_HEADS, HEAD_DIM, HEAD_DIM), lambda i: (i, 0, 0, 0))
    cbuf = pl.BlockSpec((B_SAMPLE, HIST, D_MODEL), lambda i: (i, 0, 0))
    cache = pl.BlockSpec((B_SAMPLE, N_MEM, D_MODEL), lambda i: (i, 0, 0))
    return pl.pallas_call(
        body,
        grid=(nb // B_SAMPLE,),
        in_specs=[pl.BlockSpec(memory_space=pltpu.SMEM),
                  pl.BlockSpec((rows, D_IN), lambda i: (i, 0)),
                  _const_spec((rows, HALF)), _const_spec((rows, HALF)),
                  state, cbuf, cache, cache,
                  row, _const_spec((CONV_WIDTH, D_MODEL)), row, row, row,
                  _const_spec((D_MODEL, D_MODEL)),
                  _const_spec((N_HEADS, rows, pad)),
                  _const_spec((N_HEADS, rows, HEAD_DIM)), _const_spec((N_HEADS, rows, HEAD_DIM))],
        out_specs=[pl.BlockSpec((rows, D_MODEL), lambda i: (i, 0)), state, cbuf],
        out_shape=[jax.ShapeDtypeStruct((nb * l, D_MODEL), F32),
                   jax.ShapeDtypeStruct(state_ret.shape, F32),
                   jax.ShapeDtypeStruct(state_conv.shape, F32)],
        scratch_shapes=[pltpu.VMEM((HIST + l + 6, D_MODEL), F32),
                        pltpu.VMEM((rows, D_MODEL), F32)],
        compiler_params=_params(1),
        name="sample_mixer",
    )(chunk_decay, proj, cos, sin, state_ret, state_conv, cache_k, cache_v,
      w["ret_gn_g"], w["conv_w"], w["conv_b"], w["conv_ln_g"], w["conv_ln_b"], w["w_conv_out"],
      dmat, qdec, kdec)


def kernel(x_prompt, x_sample, mem_prompt, state_ret, state_conv, cache_mem_k, cache_mem_v, g_mix, w_in, ret_gn_g, conv_w, conv_b, conv_ln_g, conv_ln_b, w_conv_out, w_out, g_ffn, w_up, w_down, g_mem, w_mem_kv, g_final):
    assert state_ret.shape[0] == 1, "one layer"
    b_p, l_p, _ = x_prompt.shape
    b_s, l_s, _ = x_sample.shape
    w = {
        "g_mix": g_mix[0][None], "w_in": w_in[0].astype(BF16), "ret_gn_g": ret_gn_g[0][None],
        "conv_w": conv_w[0], "conv_b": conv_b[0][None], "conv_ln_g": conv_ln_g[0][None],
        "conv_ln_b": conv_ln_b[0][None], "w_conv_out": w_conv_out[0].astype(BF16),
        "w_out": w_out[0].astype(BF16), "g_ffn": g_ffn[0][None], "w_up": w_up[0].astype(BF16),
        "w_down": w_down[0].astype(BF16), "g_final": g_final[None],
    }

    mk, mv, mkb, mvb = _memkv(mem_prompt, g_mem[0][None], w_mem_kv[0].astype(BF16))
    merged_p, ret_p, conv_p = _prompt_mixer(x_prompt, mkb, mvb, w)
    y_prompt = _ffn(x_prompt.reshape(b_p * l_p, D_MODEL), merged_p.reshape(b_p * l_p, D_MODEL), w)

    xs = x_sample.reshape(b_s * l_s, D_MODEL)
    proj_s = _sample_proj(xs, w)
    merged_s, ret_s, conv_s = _sample_mixer(
        proj_s, state_ret[0], state_conv[0],
        cache_mem_k[0].reshape(b_s, N_MEM, D_MODEL), cache_mem_v[0].reshape(b_s, N_MEM, D_MODEL),
        w, l_s)
    y_sample = _ffn(xs, merged_s, w)

    head_shape = (1, b_p, N_MEM, N_HEADS, HEAD_DIM)
    return (y_prompt.reshape(b_p, l_p, D_MODEL), y_sample.reshape(b_s, l_s, D_MODEL),
            ret_p[None], conv_p[None], mk.reshape(head_shape), mv.reshape(head_shape),
            ret_s[None], conv_s[None])
```

```python
import functools

import jax
import jax.numpy as jnp
from jax import lax
from jax.experimental import pallas as pl
from jax.experimental.pallas import tpu as pltpu

F32 = jnp.float32
BF16 = jnp.bfloat16

D_MODEL = 1024
N_HEADS = 4
HEAD_DIM = 256
HALF = HEAD_DIM // 2
D_FF = 4 * D_MODEL
CONV_WIDTH = 31
HIST = CONV_WIDTH - 1
N_MEM = 256
PAST_LEN = 16384
ROPE_BASE = 10000.0
EPS = 1e-6
QK_SCALE = HEAD_DIM ** -0.5

OFF_Q, OFF_K, OFF_V, OFF_G = 0, 1024, 2048, 3072
OFF_GLU_A, OFF_GLU_G, OFF_XA, OFF_GATE = 4096, 5120, 6144, 7168
D_IN = 10240

SUBLANES = 8
HIST_PAD = 32
T_PROMPT = 256
T_FFN = 256
B_SAMPLE = 4
VMEM_LIMIT = 56 * 1024 * 1024

NT_DIMS = (((1,), (1,)), ((), ()))


def _dot(a, b):
    return jnp.dot(a, b, preferred_element_type=F32)


def _dot_nt(a, b):
    return lax.dot_general(a, b, NT_DIMS, preferred_element_type=F32)


def _rms(x, g):
    return x * lax.rsqrt(jnp.mean(x * x, axis=-1, keepdims=True) + EPS) * g


def _standardize(x):
    mu = jnp.mean(x, axis=-1, keepdims=True)
    xc = x - mu
    return xc * lax.rsqrt(jnp.mean(xc * xc, axis=-1, keepdims=True) + EPS)


def _silu(x):
    return x * jax.nn.sigmoid(x)


def _rope(x, cos, sin):
    x1, x2 = x[:, :HALF], x[:, HALF:]
    return jnp.concatenate([x1 * cos - x2 * sin, x2 * cos + x1 * sin], axis=-1)


def _const_spec(shape):
    return pl.BlockSpec(shape, lambda *_: (0,) * len(shape), pipeline_mode=pl.Buffered(1))


def _params(n_grid):
    return pltpu.CompilerParams(dimension_semantics=("arbitrary",) * n_grid,
                                vmem_limit_bytes=VMEM_LIMIT)


def _memkv_kernel(mem_ref, g_ref, w_ref, k_ref, v_ref, kb_ref, vb_ref):
    h = _rms(mem_ref[0], g_ref[...]).astype(BF16)
    kv = _dot(h, w_ref[...])
    k, v = kv[:, :D_MODEL], kv[:, D_MODEL:]
    k_ref[0] = k
    v_ref[0] = v
    kb_ref[0] = k.astype(BF16)
    vb_ref[0] = v.astype(BF16)


def _memkv(mem, g_mem, w_mem_kv):
    b = mem.shape[0]
    blk = pl.BlockSpec((1, N_MEM, D_MODEL), lambda i: (i, 0, 0))
    return pl.pallas_call(
        _memkv_kernel,
        grid=(b,),
        in_specs=[blk, _const_spec((1, D_MODEL)), _const_spec((D_MODEL, 2 * D_MODEL))],
        out_specs=[blk, blk, blk, blk],
        out_shape=[jax.ShapeDtypeStruct((b, N_MEM, D_MODEL), F32)] * 2
        + [jax.ShapeDtypeStruct((b, N_MEM, D_MODEL), BF16)] * 2,
        compiler_params=_params(1),
        name="memkv",
    )(mem, g_mem, w_mem_kv)


CONV_ROWS = 128
LANES = 128


def _depthwise_conv(ext_ref, cw_ref, cb_ref, y_ref, t, lane_slices):
    first = HIST_PAD - HIST
    for j in lane_slices:
        cols = slice(j * LANES, (j + 1) * LANES)
        for r0 in range(0, t, CONV_ROWS):
            acc = jnp.broadcast_to(cb_ref[:, cols], (CONV_ROWS, LANES))
            for r in range(SUBLANES):
                rows = CONV_ROWS + (SUBLANES if r else 0)
                part = None
                for off in range(r, first + CONV_WIDTH, SUBLANES):
                    if off < first:
                        continue
                    w = off - first
                    base = r0 + off - r
                    term = ext_ref[base:base + rows, cols] * cw_ref[w:w + 1, cols]
                    part = term if part is None else part + term
                acc = acc + part[r:r + CONV_ROWS]
            y_ref[r0:r0 + CONV_ROWS, cols] = acc


def _prompt_mixer_kernel(chunk_decay,
                         x_ref, cos_ref, sin_ref, gmix_ref, win_ref, gn_ref, cw_ref, cb_ref,
                         lng_ref, lnb_ref, wco_ref, kb_ref, vb_ref, dmat_ref, qdec_ref, kdec_ref,
                         merged_ref, s_ref, cout_ref,
                         ext_scr, y_scr, part_scr, g1_scr):
    t = T_PROMPT

    @pl.when(pl.program_id(1) == 0)
    def _():
        s_ref[...] = jnp.zeros_like(s_ref)
        ext_scr[0:HIST_PAD, :] = jnp.zeros((HIST_PAD, D_MODEL), F32)

    h = _rms(x_ref[0], gmix_ref[...]).astype(BF16)

    def proj(off):
        return _dot(h, win_ref[:, off:off + HEAD_DIM])

    for j in range(D_MODEL // HEAD_DIM):
        cols = slice(j * HEAD_DIM, (j + 1) * HEAD_DIM)
        ext_scr[HIST_PAD:HIST_PAD + t, cols] = (
            proj(OFF_GLU_A + j * HEAD_DIM) * jax.nn.sigmoid(proj(OFF_GLU_G + j * HEAD_DIM)))

    cos, sin = cos_ref[...], sin_ref[...]
    conv_slices = D_MODEL // LANES // N_HEADS
    for hd in range(N_HEADS):
        sl = slice(hd * HEAD_DIM, (hd + 1) * HEAD_DIM)
        _depthwise_conv(ext_scr, cw_ref, cb_ref, y_scr, t,
                        range(hd * conv_slices, (hd + 1) * conv_slices))
        qb = _rope(proj(OFF_Q + hd * HEAD_DIM), cos, sin).astype(BF16)
        kr = _rope(proj(OFF_K + hd * HEAD_DIM), cos, sin) * QK_SCALE
        vb = proj(OFF_V + hd * HEAD_DIM).astype(BF16)
        scores = _dot_nt(qb, kr.astype(BF16)) * dmat_ref[hd]
        s_old = s_ref[0, hd]
        o = _dot(scores.astype(BF16), vb) + _dot(qb, s_old.astype(BF16)) * qdec_ref[hd]
        kd_t = (kr * kdec_ref[hd]).T.astype(BF16)
        s_ref[0, hd] = s_old * chunk_decay[hd] + _dot(kd_t, vb)
        ret = _standardize(o) * gn_ref[:, sl] * _silu(proj(OFF_G + hd * HEAD_DIM))
        qx = proj(OFF_XA + hd * HEAD_DIM).astype(BF16)
        s = _dot_nt(qx, kb_ref[0, :, sl]) * QK_SCALE
        e = jnp.exp(s - jnp.max(s, axis=-1, keepdims=True))
        p = e / jnp.sum(e, axis=-1, keepdims=True)
        xa = _dot(p.astype(BF16), vb_ref[0, :, sl])
        g0 = jax.nn.sigmoid(proj(OFF_GATE + hd * HEAD_DIM))
        g2 = jax.nn.sigmoid(proj(OFF_GATE + 2 * D_MODEL + hd * HEAD_DIM))
        part_scr[:, sl] = g0 * ret + g2 * xa
        g1_scr[:, sl] = jax.nn.sigmoid(proj(OFF_GATE + D_MODEL + hd * HEAD_DIM))

    act = _silu(_standardize(y_scr[...]) * lng_ref[...] + lnb_ref[...])
    conv = _dot(act.astype(BF16), wco_ref[...])
    merged_ref[0] = (part_scr[...] + g1_scr[...] * conv).astype(BF16)

    cout_ref[0] = ext_scr[t + HIST_PAD - HIST:t + HIST_PAD, :]
    ext_scr[0:HIST_PAD, :] = ext_scr[t:t + HIST_PAD, :]


def _decay_tables(log_gamma, c):
    idx = jnp.arange(c, dtype=F32)
    diff = idx[:, None] - idx[None, :]
    causal = diff >= 0
    intra = jnp.where(causal[None],
                      jnp.exp(log_gamma[:, None, None] * jnp.where(causal, diff, 0.0)[None]), 0.0)
    q_decay = jnp.exp(log_gamma[:, None] * (idx + 1.0))
    k_decay = jnp.exp(log_gamma[:, None] * (c - 1.0 - idx))
    chunk_decay = jnp.exp(log_gamma * c)
    return intra, q_decay, k_decay, chunk_decay


def _log_gammas():
    return jnp.log1p(-jnp.exp2(-5.0 - jnp.arange(N_HEADS, dtype=F32)))


def _rope_tables(pos):
    inv = ROPE_BASE ** (-jnp.arange(0, HEAD_DIM, 2, dtype=F32) / HEAD_DIM)
    ang = pos.astype(F32)[:, None] * inv[None, :]
    return jnp.cos(ang), jnp.sin(ang)


def _prompt_mixer(x, kb, vb, w):
    b, l, _ = x.shape
    t = T_PROMPT
    n_chunks = l // t
    cos, sin = _rope_tables(jnp.arange(l))
    intra, q_decay, k_decay, chunk_decay = _decay_tables(_log_gammas(), t)
    qdec = jnp.broadcast_to(q_decay[:, :, None], (N_HEADS, t, HEAD_DIM))
    kdec = jnp.broadcast_to(k_decay[:, :, None], (N_HEADS, t, HEAD_DIM))

    def body(cd_ref, *refs):
        cd = tuple(cd_ref[i] for i in range(N_HEADS))
        _prompt_mixer_kernel(cd, *refs)

    tok = pl.BlockSpec((1, t, D_MODEL), lambda i, j: (i, j, 0))
    rope = pl.BlockSpec((t, HALF), lambda i, j: (j, 0))
    mem = pl.BlockSpec((1, N_MEM, D_MODEL), lambda i, j: (i, 0, 0))
    row = _const_spec((1, D_MODEL))
    table = _const_spec((N_HEADS, t, HEAD_DIM))
    return pl.pallas_call(
        body,
        grid=(b, n_chunks),
        in_specs=[pl.BlockSpec(memory_space=pltpu.SMEM),
                  tok, rope, rope, row, _const_spec((D_MODEL, D_IN)), row,
                  _const_spec((CONV_WIDTH, D_MODEL)), row, row, row,
                  _const_spec((D_MODEL, D_MODEL)), mem, mem, table, table, table],
        out_specs=[tok,
                   pl.BlockSpec((1, N_HEADS, HEAD_DIM, HEAD_DIM), lambda i, j: (i, 0, 0, 0)),
                   pl.BlockSpec((1, HIST, D_MODEL), lambda i, j: (i, 0, 0))],
        out_shape=[jax.ShapeDtypeStruct((b, l, D_MODEL), BF16),
                   jax.ShapeDtypeStruct((b, N_HEADS, HEAD_DIM, HEAD_DIM), F32),
                   jax.ShapeDtypeStruct((b, HIST, D_MODEL), F32)],
        scratch_shapes=[pltpu.VMEM((HIST_PAD + t, D_MODEL), F32),
                        pltpu.VMEM((t, D_MODEL), F32),
                        pltpu.VMEM((t, D_MODEL), F32),
                        pltpu.VMEM((t, D_MODEL), F32)],
        compiler_params=_params(2),
        name="prompt_mixer",
    )(chunk_decay, x, cos, sin, w["g_mix"], w["w_in"], w["ret_gn_g"], w["conv_w"],
      w["conv_b"], w["conv_ln_g"], w["conv_ln_b"], w["w_conv_out"], kb, vb, intra, qdec, kdec)


def _ffn_kernel(x_ref, m_ref, wout_ref, gffn_ref, wup_ref, wdown_ref, gfin_ref, y_ref):
    x1 = x_ref[...] + _dot(m_ref[...].astype(BF16), wout_ref[...])
    h2 = _rms(x1, gffn_ref[...]).astype(BF16)
    a = jnp.square(jnp.maximum(_dot(h2, wup_ref[...]), 0.0)).astype(BF16)
    x2 = x1 + _dot(a, wdown_ref[...])
    y_ref[...] = _rms(x2, gfin_ref[...])


def _ffn(x, merged, w):
    m = x.shape[0]
    tok = pl.BlockSpec((T_FFN, D_MODEL), lambda i: (i, 0))
    row = _const_spec((1, D_MODEL))
    return pl.pallas_call(
        _ffn_kernel,
        grid=(m // T_FFN,),
        in_specs=[tok, tok, _const_spec((D_MODEL, D_MODEL)), row, _const_spec((D_MODEL, D_FF)),
                  _const_spec((D_FF, D_MODEL)), row],
        out_specs=tok,
        out_shape=jax.ShapeDtypeStruct((m, D_MODEL), F32),
        compiler_params=_params(1),
        name="ffn",
    )(x, merged, w["w_out"], w["g_ffn"], w["w_up"], w["w_down"], w["g_final"])


SAMPLE_PROJ_COLS = 1024


def _sample_proj_kernel(x_ref, g_ref, w_ref, o_ref):
    h = _rms(x_ref[...], g_ref[...]).astype(BF16)
    o_ref[...] = _dot(h, w_ref[...])


def _sample_proj(x, w):
    m = x.shape[0]
    return pl.pallas_call(
        _sample_proj_kernel,
        grid=(D_IN // SAMPLE_PROJ_COLS,),
        in_specs=[_const_spec((m, D_MODEL)), _const_spec((1, D_MODEL)),
                  pl.BlockSpec((D_MODEL, SAMPLE_PROJ_COLS), lambda j: (0, j))],
        out_specs=pl.BlockSpec((m, SAMPLE_PROJ_COLS), lambda j: (0, j)),
        out_shape=jax.ShapeDtypeStruct((m, D_IN), F32),
        compiler_params=_params(1),
        name="sample_proj",
    )(x, w["g_mix"], w["w_in"])


CACHE_ROWS = N_HEADS * HEAD_DIM // LANES


def _cache_rows(c):
    b = c.shape[0]
    c = c.reshape(b, N_MEM, N_HEADS, HEAD_DIM // LANES, LANES).transpose(0, 1, 3, 2, 4)
    return c.reshape(b, N_MEM * CACHE_ROWS, LANES)


def _head_rows(ref, bi, hd):
    halves = [ref[bi, pl.ds(half * N_HEADS + hd, N_MEM, stride=CACHE_ROWS), :]
              for half in range(HEAD_DIM // LANES)]
    return jnp.concatenate(halves, axis=-1)


def _sample_mixer_kernel(chunk_decay, l,
                         p_ref, cos_ref, sin_ref, s_ref, cbuf_ref, k_ref, v_ref,
                         gn_ref, cw_ref, cb_ref, lng_ref, lnb_ref, wco_ref,
                         dmat_ref, qdec_ref, kdec_ref,
                         merged_ref, sout_ref, cout_ref,
                         ext_scr, y_scr):
    rows = B_SAMPLE * l
    pad = 128
    row_id = lax.broadcasted_iota(jnp.int32, (rows, HEAD_DIM), 0)
    col_id = lax.broadcasted_iota(jnp.int32, (HEAD_DIM, pad), 1)
    row_of = [(row_id >= bi * l) & (row_id < (bi + 1) * l) for bi in range(B_SAMPLE)]
    col_of = [(col_id >= bi * l) & (col_id < (bi + 1) * l) for bi in range(B_SAMPLE)]

    def proj(off, n=HEAD_DIM):
        return p_ref[:, off:off + n]

    u = proj(OFF_GLU_A, D_MODEL) * jax.nn.sigmoid(proj(OFF_GLU_G, D_MODEL))
    for bi in range(B_SAMPLE):
        ext_scr[0:HIST, :] = cbuf_ref[bi]
        ext_scr[HIST:HIST + l, :] = u[bi * l:(bi + 1) * l, :]
        cout_ref[bi] = ext_scr[l:l + HIST, :]
        for i in range(l):
            y_scr[bi * l + i:bi * l + i + 1, :] = (
                jnp.sum(ext_scr[i:i + CONV_WIDTH, :] * cw_ref[...], axis=0, keepdims=True)
                + cb_ref[...])
    act = _silu(_standardize(y_scr[...]) * lng_ref[...] + lnb_ref[...])
    conv = _dot(act.astype(BF16), wco_ref[...])

    cos, sin = cos_ref[...], sin_ref[...]
    zeros_pad = jnp.zeros((pad - rows, HEAD_DIM), F32)
    for hd in range(N_HEADS):
        sl = slice(hd * HEAD_DIM, (hd + 1) * HEAD_DIM)
        qb = _rope(proj(OFF_Q + hd * HEAD_DIM), cos, sin).astype(BF16)
        kr = _rope(proj(OFF_K + hd * HEAD_DIM), cos, sin) * QK_SCALE
        v = proj(OFF_V + hd * HEAD_DIM)
        kb_pad = jnp.concatenate([kr, zeros_pad], axis=0).astype(BF16)
        vb_pad = jnp.concatenate([v, zeros_pad], axis=0).astype(BF16)
        scores = _dot_nt(qb, kb_pad) * dmat_ref[hd]
        o = _dot(scores.astype(BF16), vb_pad)
        kd_t = jnp.concatenate([kr * kdec_ref[hd], zeros_pad], axis=0).T
        o_cross = jnp.zeros((rows, HEAD_DIM), F32)
        for bi in range(B_SAMPLE):
            s_old = s_ref[bi, hd]
            o_cross = jnp.where(row_of[bi], _dot(qb, s_old.astype(BF16)), o_cross)
            kd_bi = jnp.where(col_of[bi], kd_t, 0.0).astype(BF16)
            sout_ref[bi, hd] = s_old * chunk_decay[hd] + _dot(kd_bi, vb_pad)
        o = o + o_cross * qdec_ref[hd]
        ret = _standardize(o) * gn_ref[:, sl] * _silu(proj(OFF_G + hd * HEAD_DIM))
        qx = proj(OFF_XA + hd * HEAD_DIM).astype(BF16)
        s = jnp.zeros((rows, N_MEM), F32)
        for bi in range(B_SAMPLE):
            s = jnp.where(row_of[bi], _dot_nt(qx, _head_rows(k_ref, bi, hd).astype(BF16)), s)
        s = s * QK_SCALE
        e = jnp.exp(s - jnp.max(s, axis=-1, keepdims=True))
        pb = (e / jnp.sum(e, axis=-1, keepdims=True)).astype(BF16)
        xa = jnp.zeros((rows, HEAD_DIM), F32)
        for bi in range(B_SAMPLE):
            xa = jnp.where(row_of[bi], _dot(pb, _head_rows(v_ref, bi, hd).astype(BF16)), xa)
        g0 = jax.nn.sigmoid(proj(OFF_GATE + hd * HEAD_DIM))
        g1 = jax.nn.sigmoid(proj(OFF_GATE + D_MODEL + hd * HEAD_DIM))
        g2 = jax.nn.sigmoid(proj(OFF_GATE + 2 * D_MODEL + hd * HEAD_DIM))
        merged_ref[:, sl] = g0 * ret + g1 * conv[:, sl] + g2 * xa


def _sample_mixer(proj, state_ret, state_conv, cache_k, cache_v, w, l):
    nb = state_ret.shape[0]
    rows = B_SAMPLE * l
    pad = 128
    cos, sin = _rope_tables(PAST_LEN + jnp.arange(l))
    cos, sin = jnp.tile(cos, (B_SAMPLE, 1)), jnp.tile(sin, (B_SAMPLE, 1))
    intra, q_decay, k_decay, chunk_decay = _decay_tables(_log_gammas(), l)
    same_req = (jnp.arange(rows)[:, None] // l) == (jnp.arange(pad)[None, :] // l)
    dmat = jnp.where(same_req[None], jnp.pad(jnp.tile(intra, (1, B_SAMPLE, B_SAMPLE)),
                                              ((0, 0), (0, 0), (0, pad - rows))), 0.0)
    qdec = jnp.broadcast_to(jnp.tile(q_decay, (1, B_SAMPLE))[:, :, None], (N_HEADS, rows, HEAD_DIM))
    kdec = jnp.broadcast_to(jnp.tile(k_decay, (1, B_SAMPLE))[:, :, None], (N_HEADS, rows, HEAD_DIM))

    def body(cd_ref, *refs):
        cd = tuple(cd_ref[i] for i in range(N_HEADS))
        _sample_mixer_kernel(cd, l, *refs)

    row = _const_spec((1, D_MODEL))
    state = pl.BlockSpec((B_SAMPLE, N_HEADS, HEAD_DIM, HEAD_DIM), lambda i: (i, 0, 0, 0))
    cbuf = pl.BlockSpec((B_SAMPLE, HIST, D_MODEL), lambda i: (i, 0, 0))
    cache = pl.BlockSpec((B_SAMPLE, N_MEM * CACHE_ROWS, LANES), lambda i: (i, 0, 0))
    return pl.pallas_call(
        body,
        grid=(nb // B_SAMPLE,),
        in_specs=[pl.BlockSpec(memory_space=pltpu.SMEM),
                  pl.BlockSpec((rows, D_IN), lambda i: (i, 0)),
                  _const_spec((rows, HALF)), _const_spec((rows, HALF)),
                  state, cbuf, cache, cache,
                  row, _const_spec((CONV_WIDTH, D_MODEL)), row, row, row,
                  _const_spec((D_MODEL, D_MODEL)),
                  _const_spec((N_HEADS, rows, pad)),
                  _const_spec((N_HEADS, rows, HEAD_DIM)), _const_spec((N_HEADS, rows, HEAD_DIM))],
        out_specs=[pl.BlockSpec((rows, D_MODEL), lambda i: (i, 0)), state, cbuf],
        out_shape=[jax.ShapeDtypeStruct((nb * l, D_MODEL), F32),
                   jax.ShapeDtypeStruct(state_ret.shape, F32),
                   jax.ShapeDtypeStruct(state_conv.shape, F32)],
        scratch_shapes=[pltpu.VMEM((HIST + l + 6, D_MODEL), F32),
                        pltpu.VMEM((rows, D_MODEL), F32)],
        compiler_params=_params(1),
        name="sample_mixer",
    )(chunk_decay, proj, cos, sin, state_ret, state_conv, cache_k, cache_v,
      w["ret_gn_g"], w["conv_w"], w["conv_b"], w["conv_ln_g"], w["conv_ln_b"], w["w_conv_out"],
      dmat, qdec, kdec)


def kernel(x_prompt, x_sample, mem_prompt, state_ret, state_conv, cache_mem_k, cache_mem_v, g_mix, w_in, ret_gn_g, conv_w, conv_b, conv_ln_g, conv_ln_b, w_conv_out, w_out, g_ffn, w_up, w_down, g_mem, w_mem_kv, g_final):
    assert state_ret.shape[0] == 1, "one layer"
    b_p, l_p, _ = x_prompt.shape
    b_s, l_s, _ = x_sample.shape
    w = {
        "g_mix": g_mix[0][None], "w_in": w_in[0].astype(BF16), "ret_gn_g": ret_gn_g[0][None],
        "conv_w": conv_w[0], "conv_b": conv_b[0][None], "conv_ln_g": conv_ln_g[0][None],
        "conv_ln_b": conv_ln_b[0][None], "w_conv_out": w_conv_out[0].astype(BF16),
        "w_out": w_out[0].astype(BF16), "g_ffn": g_ffn[0][None], "w_up": w_up[0].astype(BF16),
        "w_down": w_down[0].astype(BF16), "g_final": g_final[None],
    }

    mk, mv, mkb, mvb = _memkv(mem_prompt, g_mem[0][None], w_mem_kv[0].astype(BF16))
    merged_p, ret_p, conv_p = _prompt_mixer(x_prompt, mkb, mvb, w)
    y_prompt = _ffn(x_prompt.reshape(b_p * l_p, D_MODEL), merged_p.reshape(b_p * l_p, D_MODEL), w)

    xs = x_sample.reshape(b_s * l_s, D_MODEL)
    proj_s = _sample_proj(xs, w)
    merged_s, ret_s, conv_s = _sample_mixer(
        proj_s, state_ret[0], state_conv[0],
        _cache_rows(cache_mem_k[0]), _cache_rows(cache_mem_v[0]),
        w, l_s)
    y_sample = _ffn(xs, merged_s, w)

    head_shape = (1, b_p, N_MEM, N_HEADS, HEAD_DIM)
    return (y_prompt.reshape(b_p, l_p, D_MODEL), y_sample.reshape(b_s, l_s, D_MODEL),
            ret_p[None], conv_p[None], mk.reshape(head_shape), mv.reshape(head_shape),
            ret_s[None], conv_s[None])
```

```python
import functools

import jax
import jax.numpy as jnp
from jax import lax
from jax.experimental import pallas as pl
from jax.experimental.pallas import tpu as pltpu

F32 = jnp.float32
BF16 = jnp.bfloat16

D_MODEL = 1024
N_HEADS = 4
HEAD_DIM = 256
HALF = HEAD_DIM // 2
D_FF = 4 * D_MODEL
CONV_WIDTH = 31
HIST = CONV_WIDTH - 1
N_MEM = 256
PAST_LEN = 16384
ROPE_BASE = 10000.0
EPS = 1e-6
QK_SCALE = HEAD_DIM ** -0.5

OFF_Q, OFF_K, OFF_V, OFF_G = 0, 1024, 2048, 3072
OFF_GLU_A, OFF_GLU_G, OFF_XA, OFF_GATE = 4096, 5120, 6144, 7168
D_IN = 10240

SUBLANES = 8
HIST_PAD = 32
T_PROMPT = 256
T_FFN = 256
B_SAMPLE = 4
VMEM_LIMIT = 56 * 1024 * 1024

LANES = 128
CACHE_ROWS = N_HEADS * HEAD_DIM // LANES

NT_DIMS = (((1,), (1,)), ((), ()))


def _cache_rows(c):
    b = c.shape[0]
    c = c.reshape(b, N_MEM, N_HEADS, HEAD_DIM // LANES, LANES).transpose(0, 1, 3, 2, 4)
    return c.reshape(b, N_MEM * CACHE_ROWS, LANES)


def _cache_from_rows(r):
    b = r.shape[0]
    r = r.reshape(b, N_MEM, HEAD_DIM // LANES, N_HEADS, LANES).transpose(0, 1, 3, 2, 4)
    return r.reshape(b, N_MEM, N_HEADS, HEAD_DIM)


def _dot(a, b):
    return jnp.dot(a, b, preferred_element_type=F32)


def _dot_nt(a, b):
    return lax.dot_general(a, b, NT_DIMS, preferred_element_type=F32)


def _rms(x, g):
    return x * lax.rsqrt(jnp.mean(x * x, axis=-1, keepdims=True) + EPS) * g


def _standardize(x):
    mu = jnp.mean(x, axis=-1, keepdims=True)
    xc = x - mu
    return xc * lax.rsqrt(jnp.mean(xc * xc, axis=-1, keepdims=True) + EPS)


def _silu(x):
    return x * jax.nn.sigmoid(x)


def _rope(x, cos, sin):
    x1, x2 = x[:, :HALF], x[:, HALF:]
    return jnp.concatenate([x1 * cos - x2 * sin, x2 * cos + x1 * sin], axis=-1)


def _const_spec(shape):
    return pl.BlockSpec(shape, lambda *_: (0,) * len(shape), pipeline_mode=pl.Buffered(1))


def _params(n_grid):
    return pltpu.CompilerParams(dimension_semantics=("arbitrary",) * n_grid,
                                vmem_limit_bytes=VMEM_LIMIT)


def _memkv_kernel(mem_ref, g_ref, w_ref, k_ref, v_ref, kb_ref, vb_ref):
    h = _rms(mem_ref[0], g_ref[...]).astype(BF16)
    kv = _dot(h, w_ref[...])
    k, v = kv[:, :D_MODEL], kv[:, D_MODEL:]
    kb_ref[0] = k.astype(BF16)
    vb_ref[0] = v.astype(BF16)
    for hd in range(N_HEADS):
        for half in range(HEAD_DIM // LANES):
            rows = pl.ds(half * N_HEADS + hd, N_MEM, stride=CACHE_ROWS)
            cols = slice(hd * HEAD_DIM + half * LANES, hd * HEAD_DIM + (half + 1) * LANES)
            k_ref[0, rows, :] = k[:, cols]
            v_ref[0, rows, :] = v[:, cols]


def _memkv(mem, g_mem, w_mem_kv):
    b = mem.shape[0]
    blk = pl.BlockSpec((1, N_MEM, D_MODEL), lambda i: (i, 0, 0))
    rows = pl.BlockSpec((1, N_MEM * CACHE_ROWS, LANES), lambda i: (i, 0, 0))
    k, v, kb, vb = pl.pallas_call(
        _memkv_kernel,
        grid=(b,),
        in_specs=[blk, _const_spec((1, D_MODEL)), _const_spec((D_MODEL, 2 * D_MODEL))],
        out_specs=[rows, rows, blk, blk],
        out_shape=[jax.ShapeDtypeStruct((b, N_MEM * CACHE_ROWS, LANES), F32)] * 2
        + [jax.ShapeDtypeStruct((b, N_MEM, D_MODEL), BF16)] * 2,
        compiler_params=_params(1),
        name="memkv",
    )(mem, g_mem, w_mem_kv)
    return _cache_from_rows(k), _cache_from_rows(v), kb, vb


CONV_ROWS = 128
CONV_SPLIT = (3, 3, 2)


def _depthwise_conv(ext_ref, cw_ref, cb_ref, y_ref, t, lane_slices):
    first = HIST_PAD - HIST
    for j in lane_slices:
        cols = slice(j * LANES, (j + 1) * LANES)
        for r0 in range(0, t, CONV_ROWS):
            acc = jnp.broadcast_to(cb_ref[:, cols], (CONV_ROWS, LANES))
            for r in range(SUBLANES):
                rows = CONV_ROWS + (SUBLANES if r else 0)
                part = None
                for off in range(r, first + CONV_WIDTH, SUBLANES):
                    if off < first:
                        continue
                    w = off - first
                    base = r0 + off - r
                    term = ext_ref[base:base + rows, cols] * cw_ref[w:w + 1, cols]
                    part = term if part is None else part + term
                acc = acc + part[r:r + CONV_ROWS]
            y_ref[r0:r0 + CONV_ROWS, cols] = acc


def _prompt_mixer_kernel(chunk_decay,
                         x_ref, cos_ref, sin_ref, gmix_ref, win_ref, gn_ref, cw_ref, cb_ref,
                         lng_ref, lnb_ref, wco_ref, kb_ref, vb_ref, dmat_ref, qdec_ref, kdec_ref,
                         merged_ref, s_ref, cout_ref,
                         ext_scr, y_scr, conv_scr, part_scr, g1_scr):
    t = T_PROMPT

    @pl.when(pl.program_id(1) == 0)
    def _():
        s_ref[...] = jnp.zeros_like(s_ref)
        ext_scr[0:HIST_PAD, :] = jnp.zeros((HIST_PAD, D_MODEL), F32)

    h = _rms(x_ref[0], gmix_ref[...]).astype(BF16)

    def proj(off):
        return _dot(h, win_ref[:, off:off + HEAD_DIM])

    for j in range(D_MODEL // HEAD_DIM):
        cols = slice(j * HEAD_DIM, (j + 1) * HEAD_DIM)
        ext_scr[HIST_PAD:HIST_PAD + t, cols] = (
            proj(OFF_GLU_A + j * HEAD_DIM) * jax.nn.sigmoid(proj(OFF_GLU_G + j * HEAD_DIM)))

    cos, sin = cos_ref[...], sin_ref[...]
    assert sum(CONV_SPLIT) == D_MODEL // LANES and len(CONV_SPLIT) < N_HEADS
    for hd in range(N_HEADS):
        sl = slice(hd * HEAD_DIM, (hd + 1) * HEAD_DIM)
        if hd < len(CONV_SPLIT):
            first_slice = sum(CONV_SPLIT[:hd])
            _depthwise_conv(ext_scr, cw_ref, cb_ref, y_scr, t,
                            range(first_slice, first_slice + CONV_SPLIT[hd]))
        elif hd == len(CONV_SPLIT):
            act = _silu(_standardize(y_scr[...]) * lng_ref[...] + lnb_ref[...])
            conv_scr[...] = _dot(act.astype(BF16), wco_ref[...])
        qb = _rope(proj(OFF_Q + hd * HEAD_DIM), cos, sin).astype(BF16)
        kr = _rope(proj(OFF_K + hd * HEAD_DIM), cos, sin) * QK_SCALE
        vb = proj(OFF_V + hd * HEAD_DIM).astype(BF16)
        swish_g = _silu(proj(OFF_G + hd * HEAD_DIM))
        scores = _dot_nt(qb, kr.astype(BF16)) * dmat_ref[hd]
        qx = proj(OFF_XA + hd * HEAD_DIM).astype(BF16)
        s_old = s_ref[0, hd]
        o_cross = _dot(qb, s_old.astype(BF16)) * qdec_ref[hd]
        g0 = jax.nn.sigmoid(proj(OFF_GATE + hd * HEAD_DIM))
        o = _dot(scores.astype(BF16), vb) + o_cross
        s = _dot_nt(qx, kb_ref[0, :, sl]) * QK_SCALE
        g2 = jax.nn.sigmoid(proj(OFF_GATE + 2 * D_MODEL + hd * HEAD_DIM))
        kd_t = (kr * kdec_ref[hd]).T.astype(BF16)
        s_ref[0, hd] = s_old * chunk_decay[hd] + _dot(kd_t, vb)
        g1_scr[:, sl] = jax.nn.sigmoid(proj(OFF_GATE + D_MODEL + hd * HEAD_DIM))
        e = jnp.exp(s - jnp.max(s, axis=-1, keepdims=True))
        p = e / jnp.sum(e, axis=-1, keepdims=True)
        xa = _dot(p.astype(BF16), vb_ref[0, :, sl])
        ret = _standardize(o) * gn_ref[:, sl] * swish_g
        part_scr[:, sl] = g0 * ret + g2 * xa

    merged_ref[0] = (part_scr[...] + g1_scr[...] * conv_scr[...]).astype(BF16)

    cout_ref[0] = ext_scr[t + HIST_PAD - HIST:t + HIST_PAD, :]
    ext_scr[0:HIST_PAD, :] = ext_scr[t:t + HIST_PAD, :]


def _decay_tables(log_gamma, c):
    idx = jnp.arange(c, dtype=F32)
    diff = idx[:, None] - idx[None, :]
    causal = diff >= 0
    intra = jnp.where(causal[None],
                      jnp.exp(log_gamma[:, None, None] * jnp.where(causal, diff, 0.0)[None]), 0.0)
    q_decay = jnp.exp(log_gamma[:, None] * (idx + 1.0))
    k_decay = jnp.exp(log_gamma[:, None] * (c - 1.0 - idx))
    chunk_decay = jnp.exp(log_gamma * c)
    return intra, q_decay, k_decay, chunk_decay


def _log_gammas():
    return jnp.log1p(-jnp.exp2(-5.0 - jnp.arange(N_HEADS, dtype=F32)))


def _rope_tables(pos):
    inv = ROPE_BASE ** (-jnp.arange(0, HEAD_DIM, 2, dtype=F32) / HEAD_DIM)
    ang = pos.astype(F32)[:, None] * inv[None, :]
    return jnp.cos(ang), jnp.sin(ang)


def _prompt_mixer(x, kb, vb, w):
    b, l, _ = x.shape
    t = T_PROMPT
    n_chunks = l // t
    cos, sin = _rope_tables(jnp.arange(l))
    intra, q_decay, k_decay, chunk_decay = _decay_tables(_log_gammas(), t)
    qdec = jnp.broadcast_to(q_decay[:, :, None], (N_HEADS, t, HEAD_DIM))
    kdec = jnp.broadcast_to(k_decay[:, :, None], (N_HEADS, t, HEAD_DIM))

    def body(cd_ref, *refs):
        cd = tuple(cd_ref[i] for i in range(N_HEADS))
        _prompt_mixer_kernel(cd, *refs)

    tok = pl.BlockSpec((1, t, D_MODEL), lambda i, j: (i, j, 0))
    rope = pl.BlockSpec((t, HALF), lambda i, j: (j, 0))
    mem = pl.BlockSpec((1, N_MEM, D_MODEL), lambda i, j: (i, 0, 0))
    row = _const_spec((1, D_MODEL))
    table = _const_spec((N_HEADS, t, HEAD_DIM))
    return pl.pallas_call(
        body,
        grid=(b, n_chunks),
        in_specs=[pl.BlockSpec(memory_space=pltpu.SMEM),
                  tok, rope, rope, row, _const_spec((D_MODEL, D_IN)), row,
                  _const_spec((CONV_WIDTH, D_MODEL)), row, row, row,
                  _const_spec((D_MODEL, D_MODEL)), mem, mem, table, table, table],
        out_specs=[tok,
                   pl.BlockSpec((1, N_HEADS, HEAD_DIM, HEAD_DIM), lambda i, j: (i, 0, 0, 0)),
                   pl.BlockSpec((1, HIST, D_MODEL), lambda i, j: (i, 0, 0))],
        out_shape=[jax.ShapeDtypeStruct((b, l, D_MODEL), BF16),
                   jax.ShapeDtypeStruct((b, N_HEADS, HEAD_DIM, HEAD_DIM), F32),
                   jax.ShapeDtypeStruct((b, HIST, D_MODEL), F32)],
        scratch_shapes=[pltpu.VMEM((HIST_PAD + t, D_MODEL), F32)]
        + [pltpu.VMEM((t, D_MODEL), F32)] * 4,
        compiler_params=_params(2),
        name="prompt_mixer",
    )(chunk_decay, x, cos, sin, w["g_mix"], w["w_in"], w["ret_gn_g"], w["conv_w"],
      w["conv_b"], w["conv_ln_g"], w["conv_ln_b"], w["w_conv_out"], kb, vb, intra, qdec, kdec)


def _ffn_kernel(x_ref, m_ref, wout_ref, gffn_ref, wup_ref, wdown_ref, gfin_ref, y_ref):
    x1 = x_ref[...] + _dot(m_ref[...].astype(BF16), wout_ref[...])
    h2 = _rms(x1, gffn_ref[...]).astype(BF16)
    a = jnp.square(jnp.maximum(_dot(h2, wup_ref[...]), 0.0)).astype(BF16)
    x2 = x1 + _dot(a, wdown_ref[...])
    y_ref[...] = _rms(x2, gfin_ref[...])


def _ffn(x, merged, w):
    m = x.shape[0]
    tok = pl.BlockSpec((T_FFN, D_MODEL), lambda i: (i, 0))
    row = _const_spec((1, D_MODEL))
    return pl.pallas_call(
        _ffn_kernel,
        grid=(m // T_FFN,),
        in_specs=[tok, tok, _const_spec((D_MODEL, D_MODEL)), row, _const_spec((D_MODEL, D_FF)),
                  _const_spec((D_FF, D_MODEL)), row],
        out_specs=tok,
        out_shape=jax.ShapeDtypeStruct((m, D_MODEL), F32),
        compiler_params=_params(1),
        name="ffn",
    )(x, merged, w["w_out"], w["g_ffn"], w["w_up"], w["w_down"], w["g_final"])


SAMPLE_PROJ_COLS = 1024


def _sample_proj_kernel(x_ref, g_ref, w_ref, o_ref):
    h = _rms(x_ref[...], g_ref[...]).astype(BF16)
    o_ref[...] = _dot(h, w_ref[...])


def _sample_proj(x, w):
    m = x.shape[0]
    return pl.pallas_call(
        _sample_proj_kernel,
        grid=(D_IN // SAMPLE_PROJ_COLS,),
        in_specs=[_const_spec((m, D_MODEL)), _const_spec((1, D_MODEL)),
                  pl.BlockSpec((D_MODEL, SAMPLE_PROJ_COLS), lambda j: (0, j))],
        out_specs=pl.BlockSpec((m, SAMPLE_PROJ_COLS), lambda j: (0, j)),
        out_shape=jax.ShapeDtypeStruct((m, D_IN), F32),
        compiler_params=_params(1),
        name="sample_proj",
    )(x, w["g_mix"], w["w_in"])


def _head_rows(ref, bi, hd):
    halves = [ref[bi, pl.ds(half * N_HEADS + hd, N_MEM, stride=CACHE_ROWS), :]
              for half in range(HEAD_DIM // LANES)]
    return jnp.concatenate(halves, axis=-1)


def _sample_mixer_kernel(chunk_decay, l,
                         p_ref, cos_ref, sin_ref, s_ref, cbuf_ref, k_ref, v_ref,
                         gn_ref, cw_ref, cb_ref, lng_ref, lnb_ref, wco_ref,
                         dmat_ref, qdec_ref, kdec_ref,
                         merged_ref, sout_ref, cout_ref,
                         ext_scr, y_scr):
    rows = B_SAMPLE * l
    pad = 128
    row_id = lax.broadcasted_iota(jnp.int32, (rows, HEAD_DIM), 0)
    col_id = lax.broadcasted_iota(jnp.int32, (HEAD_DIM, pad), 1)
    row_of = [(row_id >= bi * l) & (row_id < (bi + 1) * l) for bi in range(B_SAMPLE)]
    col_of = [(col_id >= bi * l) & (col_id < (bi + 1) * l) for bi in range(B_SAMPLE)]

    def proj(off, n=HEAD_DIM):
        return p_ref[:, off:off + n]

    u = proj(OFF_GLU_A, D_MODEL) * jax.nn.sigmoid(proj(OFF_GLU_G, D_MODEL))
    for bi in range(B_SAMPLE):
        ext_scr[0:HIST, :] = cbuf_ref[bi]
        ext_scr[HIST:HIST + l, :] = u[bi * l:(bi + 1) * l, :]
        cout_ref[bi] = ext_scr[l:l + HIST, :]
        for i in range(l):
            y_scr[bi * l + i:bi * l + i + 1, :] = (
                jnp.sum(ext_scr[i:i + CONV_WIDTH, :] * cw_ref[...], axis=0, keepdims=True)
                + cb_ref[...])
    act = _silu(_standardize(y_scr[...]) * lng_ref[...] + lnb_ref[...])
    conv = _dot(act.astype(BF16), wco_ref[...])

    cos, sin = cos_ref[...], sin_ref[...]
    zeros_pad = jnp.zeros((pad - rows, HEAD_DIM), F32)
    for hd in range(N_HEADS):
        sl = slice(hd * HEAD_DIM, (hd + 1) * HEAD_DIM)
        qb = _rope(proj(OFF_Q + hd * HEAD_DIM), cos, sin).astype(BF16)
        kr = _rope(proj(OFF_K + hd * HEAD_DIM), cos, sin) * QK_SCALE
        v = proj(OFF_V + hd * HEAD_DIM)
        kb_pad = jnp.concatenate([kr, zeros_pad], axis=0).astype(BF16)
        vb_pad = jnp.concatenate([v, zeros_pad], axis=0).astype(BF16)
        scores = _dot_nt(qb, kb_pad) * dmat_ref[hd]
        o = _dot(scores.astype(BF16), vb_pad)
        kd_t = jnp.concatenate([kr * kdec_ref[hd], zeros_pad], axis=0).T
        o_cross = jnp.zeros((rows, HEAD_DIM), F32)
        for bi in range(B_SAMPLE):
            s_old = s_ref[bi, hd]
            o_cross = jnp.where(row_of[bi], _dot(qb, s_old.astype(BF16)), o_cross)
            kd_bi = jnp.where(col_of[bi], kd_t, 0.0).astype(BF16)
            sout_ref[bi, hd] = s_old * chunk_decay[hd] + _dot(kd_bi, vb_pad)
        o = o + o_cross * qdec_ref[hd]
        ret = _standardize(o) * gn_ref[:, sl] * _silu(proj(OFF_G + hd * HEAD_DIM))
        qx = proj(OFF_XA + hd * HEAD_DIM).astype(BF16)
        s = jnp.zeros((rows, N_MEM), F32)
        for bi in range(B_SAMPLE):
            s = jnp.where(row_of[bi], _dot_nt(qx, _head_rows(k_ref, bi, hd).astype(BF16)), s)
        s = s * QK_SCALE
        e = jnp.exp(s - jnp.max(s, axis=-1, keepdims=True))
        pb = (e / jnp.sum(e, axis=-1, keepdims=True)).astype(BF16)
        xa = jnp.zeros((rows, HEAD_DIM), F32)
        for bi in range(B_SAMPLE):
            xa = jnp.where(row_of[bi], _dot(pb, _head_rows(v_ref, bi, hd).astype(BF16)), xa)
        g0 = jax.nn.sigmoid(proj(OFF_GATE + hd * HEAD_DIM))
        g1 = jax.nn.sigmoid(proj(OFF_GATE + D_MODEL + hd * HEAD_DIM))
        g2 = jax.nn.sigmoid(proj(OFF_GATE + 2 * D_MODEL + hd * HEAD_DIM))
        merged_ref[:, sl] = g0 * ret + g1 * conv[:, sl] + g2 * xa


def _sample_mixer(proj, state_ret, state_conv, cache_k, cache_v, w, l):
    nb = state_ret.shape[0]
    rows = B_SAMPLE * l
    pad = 128
    cos, sin = _rope_tables(PAST_LEN + jnp.arange(l))
    cos, sin = jnp.tile(cos, (B_SAMPLE, 1)), jnp.tile(sin, (B_SAMPLE, 1))
    intra, q_decay, k_decay, chunk_decay = _decay_tables(_log_gammas(), l)
    same_req = (jnp.arange(rows)[:, None] // l) == (jnp.arange(pad)[None, :] // l)
    dmat = jnp.where(same_req[None], jnp.pad(jnp.tile(intra, (1, B_SAMPLE, B_SAMPLE)),
                                              ((0, 0), (0, 0), (0, pad - rows))), 0.0)
    qdec = jnp.broadcast_to(jnp.tile(q_decay, (1, B_SAMPLE))[:, :, None], (N_HEADS, rows, HEAD_DIM))
    kdec = jnp.broadcast_to(jnp.tile(k_decay, (1, B_SAMPLE))[:, :, None], (N_HEADS, rows, HEAD_DIM))

    def body(cd_ref, *refs):
        cd = tuple(cd_ref[i] for i in range(N_HEADS))
        _sample_mixer_kernel(cd, l, *refs)

    row = _const_spec((1, D_MODEL))
    state = pl.BlockSpec((B_SAMPLE, N_HEADS, HEAD_DIM, HEAD_DIM), lambda i: (i, 0, 0, 0))
    cbuf = pl.BlockSpec((B_SAMPLE, HIST, D_MODEL), lambda i: (i, 0, 0))
    cache = pl.BlockSpec((B_SAMPLE, N_MEM * CACHE_ROWS, LANES), lambda i: (i, 0, 0))
    return pl.pallas_call(
        body,
        grid=(nb // B_SAMPLE,),
        in_specs=[pl.BlockSpec(memory_space=pltpu.SMEM),
                  pl.BlockSpec((rows, D_IN), lambda i: (i, 0)),
                  _const_spec((rows, HALF)), _const_spec((rows, HALF)),
                  state, cbuf, cache, cache,
                  row, _const_spec((CONV_WIDTH, D_MODEL)), row, row, row,
                  _const_spec((D_MODEL, D_MODEL)),
                  _const_spec((N_HEADS, rows, pad)),
                  _const_spec((N_HEADS, rows, HEAD_DIM)), _const_spec((N_HEADS, rows, HEAD_DIM))],
        out_specs=[pl.BlockSpec((rows, D_MODEL), lambda i: (i, 0)), state, cbuf],
        out_shape=[jax.ShapeDtypeStruct((nb * l, D_MODEL), F32),
                   jax.ShapeDtypeStruct(state_ret.shape, F32),
                   jax.ShapeDtypeStruct(state_conv.shape, F32)],
        scratch_shapes=[pltpu.VMEM((HIST + l + 6, D_MODEL), F32),
                        pltpu.VMEM((rows, D_MODEL), F32)],
        compiler_params=_params(1),
        name="sample_mixer",
    )(chunk_decay, proj, cos, sin, state_ret, state_conv, cache_k, cache_v,
      w["ret_gn_g"], w["conv_w"], w["conv_b"], w["conv_ln_g"], w["conv_ln_b"], w["w_conv_out"],
      dmat, qdec, kdec)


def kernel(x_prompt, x_sample, mem_prompt, state_ret, state_conv, cache_mem_k, cache_mem_v, g_mix, w_in, ret_gn_g, conv_w, conv_b, conv_ln_g, conv_ln_b, w_conv_out, w_out, g_ffn, w_up, w_down, g_mem, w_mem_kv, g_final):
    assert state_ret.shape[0] == 1, "one layer"
    b_p, l_p, _ = x_prompt.shape
    b_s, l_s, _ = x_sample.shape
    w = {
        "g_mix": g_mix[0][None], "w_in": w_in[0].astype(BF16), "ret_gn_g": ret_gn_g[0][None],
        "conv_w": conv_w[0], "conv_b": conv_b[0][None], "conv_ln_g": conv_ln_g[0][None],
        "conv_ln_b": conv_ln_b[0][None], "w_conv_out": w_conv_out[0].astype(BF16),
        "w_out": w_out[0].astype(BF16), "g_ffn": g_ffn[0][None], "w_up": w_up[0].astype(BF16),
        "w_down": w_down[0].astype(BF16), "g_final": g_final[None],
    }

    mk, mv, mkb, mvb = _memkv(mem_prompt, g_mem[0][None], w_mem_kv[0].astype(BF16))
    merged_p, ret_p, conv_p = _prompt_mixer(x_prompt, mkb, mvb, w)
    y_prompt = _ffn(x_prompt.reshape(b_p * l_p, D_MODEL), merged_p.reshape(b_p * l_p, D_MODEL), w)

    xs = x_sample.reshape(b_s * l_s, D_MODEL)
    proj_s = _sample_proj(xs, w)
    merged_s, ret_s, conv_s = _sample_mixer(
        proj_s, state_ret[0], state_conv[0],
        _cache_rows(cache_mem_k[0]), _cache_rows(cache_mem_v[0]),
        w, l_s)
    y_sample = _ffn(xs, merged_s, w)

    return (y_prompt.reshape(b_p, l_p, D_MODEL), y_sample.reshape(b_s, l_s, D_MODEL),
            ret_p[None], conv_p[None], mk[None], mv[None], ret_s[None], conv_s[None])
```

```python
import functools

import jax
import jax.numpy as jnp
from jax import lax
from jax.experimental import pallas as pl
from jax.experimental.pallas import tpu as pltpu

F32 = jnp.float32
BF16 = jnp.bfloat16

D_MODEL = 1024
N_HEADS = 4
HEAD_DIM = 256
HALF = HEAD_DIM // 2
D_FF = 4 * D_MODEL
CONV_WIDTH = 31
HIST = CONV_WIDTH - 1
N_MEM = 256
PAST_LEN = 16384
ROPE_BASE = 10000.0
EPS = 1e-6
QK_SCALE = HEAD_DIM ** -0.5

OFF_Q, OFF_K, OFF_V, OFF_G = 0, 1024, 2048, 3072
OFF_GLU_A, OFF_GLU_G, OFF_XA, OFF_GATE = 4096, 5120, 6144, 7168
D_IN = 10240

SUBLANES = 8
HIST_PAD = 32
T_PROMPT = 256
T_FFN = 512
FFN_ROWS = 256
B_SAMPLE = 4
VMEM_LIMIT = 56 * 1024 * 1024

LANES = 128
CACHE_ROWS = N_HEADS * HEAD_DIM // LANES

NT_DIMS = (((1,), (1,)), ((), ()))


def _cache_rows(c):
    b = c.shape[0]
    c = c.reshape(b, N_MEM, N_HEADS, HEAD_DIM // LANES, LANES).transpose(0, 1, 3, 2, 4)
    return c.reshape(b, N_MEM * CACHE_ROWS, LANES)


def _cache_from_rows(r):
    b = r.shape[0]
    r = r.reshape(b, N_MEM, HEAD_DIM // LANES, N_HEADS, LANES).transpose(0, 1, 3, 2, 4)
    return r.reshape(b, N_MEM, N_HEADS, HEAD_DIM)


def _dot(a, b):
    return jnp.dot(a, b, preferred_element_type=F32)


def _dot_nt(a, b):
    return lax.dot_general(a, b, NT_DIMS, preferred_element_type=F32)


def _rms(x, g):
    return x * lax.rsqrt(jnp.mean(x * x, axis=-1, keepdims=True) + EPS) * g


def _standardize(x):
    mu = jnp.mean(x, axis=-1, keepdims=True)
    xc = x - mu
    return xc * lax.rsqrt(jnp.mean(xc * xc, axis=-1, keepdims=True) + EPS)


def _silu(x):
    return x * jax.nn.sigmoid(x)


def _rope(x, cos, sin):
    x1, x2 = x[:, :HALF], x[:, HALF:]
    return jnp.concatenate([x1 * cos - x2 * sin, x2 * cos + x1 * sin], axis=-1)


def _const_spec(shape):
    return pl.BlockSpec(shape, lambda *_: (0,) * len(shape), pipeline_mode=pl.Buffered(1))


def _params(n_grid):
    return pltpu.CompilerParams(dimension_semantics=("arbitrary",) * n_grid,
                                vmem_limit_bytes=VMEM_LIMIT)


def _memkv_kernel(mem_ref, g_ref, w_ref, k_ref, v_ref, kb_ref, vb_ref):
    h = _rms(mem_ref[0], g_ref[...]).astype(BF16)
    kv = _dot(h, w_ref[...])
    k, v = kv[:, :D_MODEL], kv[:, D_MODEL:]
    kb_ref[0] = k.astype(BF16)
    vb_ref[0] = v.astype(BF16)
    for hd in range(N_HEADS):
        for half in range(HEAD_DIM // LANES):
            rows = pl.ds(half * N_HEADS + hd, N_MEM, stride=CACHE_ROWS)
            cols = slice(hd * HEAD_DIM + half * LANES, hd * HEAD_DIM + (half + 1) * LANES)
            k_ref[0, rows, :] = k[:, cols]
            v_ref[0, rows, :] = v[:, cols]


def _memkv(mem, g_mem, w_mem_kv):
    b = mem.shape[0]
    blk = pl.BlockSpec((1, N_MEM, D_MODEL), lambda i: (i, 0, 0))
    rows = pl.BlockSpec((1, N_MEM * CACHE_ROWS, LANES), lambda i: (i, 0, 0))
    k, v, kb, vb = pl.pallas_call(
        _memkv_kernel,
        grid=(b,),
        in_specs=[blk, _const_spec((1, D_MODEL)), _const_spec((D_MODEL, 2 * D_MODEL))],
        out_specs=[rows, rows, blk, blk],
        out_shape=[jax.ShapeDtypeStruct((b, N_MEM * CACHE_ROWS, LANES), F32)] * 2
        + [jax.ShapeDtypeStruct((b, N_MEM, D_MODEL), BF16)] * 2,
        compiler_params=_params(1),
        name="memkv",
    )(mem, g_mem, w_mem_kv)
    return _cache_from_rows(k), _cache_from_rows(v), kb, vb


CONV_ROWS = 128
CONV_SPLIT = (3, 3, 2)


def _depthwise_conv(ext_ref, cw_ref, cb_ref, y_ref, t, lane_slices):
    first = HIST_PAD - HIST
    for j in lane_slices:
        cols = slice(j * LANES, (j + 1) * LANES)
        for r0 in range(0, t, CONV_ROWS):
            acc = jnp.broadcast_to(cb_ref[:, cols], (CONV_ROWS, LANES))
            for r in range(SUBLANES):
                rows = CONV_ROWS + (SUBLANES if r else 0)
                part = None
                for off in range(r, first + CONV_WIDTH, SUBLANES):
                    if off < first:
                        continue
                    w = off - first
                    base = r0 + off - r
                    term = ext_ref[base:base + rows, cols] * cw_ref[w:w + 1, cols]
                    part = term if part is None else part + term
                acc = acc + part[r:r + CONV_ROWS]
            y_ref[r0:r0 + CONV_ROWS, cols] = acc


def _prompt_mixer_kernel(chunk_decay,
                         x_ref, cos_ref, sin_ref, gmix_ref, win_ref, gn_ref, cw_ref, cb_ref,
                         lng_ref, lnb_ref, wco_ref, kb_ref, vb_ref, dmat_ref, qdec_ref, kdec_ref,
                         merged_ref, s_ref, cout_ref,
                         ext_scr, y_scr, conv_scr, part_scr, g1_scr):
    t = T_PROMPT

    @pl.when(pl.program_id(1) == 0)
    def _():
        s_ref[...] = jnp.zeros_like(s_ref)
        ext_scr[0:HIST_PAD, :] = jnp.zeros((HIST_PAD, D_MODEL), F32)

    h = _rms(x_ref[0], gmix_ref[...]).astype(BF16)

    def proj(off):
        return _dot(h, win_ref[:, off:off + HEAD_DIM])

    for j in range(D_MODEL // HEAD_DIM):
        cols = slice(j * HEAD_DIM, (j + 1) * HEAD_DIM)
        ext_scr[HIST_PAD:HIST_PAD + t, cols] = (
            proj(OFF_GLU_A + j * HEAD_DIM) * jax.nn.sigmoid(proj(OFF_GLU_G + j * HEAD_DIM)))

    cos, sin = cos_ref[...], sin_ref[...]
    assert sum(CONV_SPLIT) == D_MODEL // LANES and len(CONV_SPLIT) < N_HEADS
    for hd in range(N_HEADS):
        sl = slice(hd * HEAD_DIM, (hd + 1) * HEAD_DIM)
        if hd < len(CONV_SPLIT):
            first_slice = sum(CONV_SPLIT[:hd])
            _depthwise_conv(ext_scr, cw_ref, cb_ref, y_scr, t,
                            range(first_slice, first_slice + CONV_SPLIT[hd]))
        elif hd == len(CONV_SPLIT):
            act = _silu(_standardize(y_scr[...]) * lng_ref[...] + lnb_ref[...])
            conv_scr[...] = _dot(act.astype(BF16), wco_ref[...])
            done = slice(0, hd * HEAD_DIM)
            merged_ref[0, :, done] = (
                part_scr[:, done] + g1_scr[:, done] * conv_scr[:, done]).astype(BF16)
        qb = _rope(proj(OFF_Q + hd * HEAD_DIM), cos, sin).astype(BF16)
        kr = _rope(proj(OFF_K + hd * HEAD_DIM), cos, sin) * QK_SCALE
        vb = proj(OFF_V + hd * HEAD_DIM).astype(BF16)
        swish_g = _silu(proj(OFF_G + hd * HEAD_DIM))
        scores = _dot_nt(qb, kr.astype(BF16)) * dmat_ref[hd]
        qx = proj(OFF_XA + hd * HEAD_DIM).astype(BF16)
        s_old = s_ref[0, hd]
        o_cross = _dot(qb, s_old.astype(BF16)) * qdec_ref[hd]
        g0 = jax.nn.sigmoid(proj(OFF_GATE + hd * HEAD_DIM))
        o = _dot(scores.astype(BF16), vb) + o_cross
        s = _dot_nt(qx, kb_ref[0, :, sl]) * QK_SCALE
        g2 = jax.nn.sigmoid(proj(OFF_GATE + 2 * D_MODEL + hd * HEAD_DIM))
        kd_t = (kr * kdec_ref[hd]).T.astype(BF16)
        s_ref[0, hd] = s_old * chunk_decay[hd] + _dot(kd_t, vb)
        g1 = jax.nn.sigmoid(proj(OFF_GATE + D_MODEL + hd * HEAD_DIM))
        e = jnp.exp(s - jnp.max(s, axis=-1, keepdims=True))
        p = e / jnp.sum(e, axis=-1, keepdims=True)
        xa = _dot(p.astype(BF16), vb_ref[0, :, sl])
        ret = _standardize(o) * gn_ref[:, sl] * swish_g
        part = g0 * ret + g2 * xa
        if hd < len(CONV_SPLIT):
            part_scr[:, sl] = part
            g1_scr[:, sl] = g1
        else:
            merged_ref[0, :, sl] = (part + g1 * conv_scr[:, sl]).astype(BF16)

    cout_ref[0] = ext_scr[t + HIST_PAD - HIST:t + HIST_PAD, :]
    ext_scr[0:HIST_PAD, :] = ext_scr[t:t + HIST_PAD, :]


def _decay_tables(log_gamma, c):
    idx = jnp.arange(c, dtype=F32)
    diff = idx[:, None] - idx[None, :]
    causal = diff >= 0
    intra = jnp.where(causal[None],
                      jnp.exp(log_gamma[:, None, None] * jnp.where(causal, diff, 0.0)[None]), 0.0)
    q_decay = jnp.exp(log_gamma[:, None] * (idx + 1.0))
    k_decay = jnp.exp(log_gamma[:, None] * (c - 1.0 - idx))
    chunk_decay = jnp.exp(log_gamma * c)
    return intra, q_decay, k_decay, chunk_decay


def _log_gammas():
    return jnp.log1p(-jnp.exp2(-5.0 - jnp.arange(N_HEADS, dtype=F32)))


def _rope_tables(pos):
    inv = ROPE_BASE ** (-jnp.arange(0, HEAD_DIM, 2, dtype=F32) / HEAD_DIM)
    ang = pos.astype(F32)[:, None] * inv[None, :]
    return jnp.cos(ang), jnp.sin(ang)


def _prompt_mixer(x, kb, vb, w):
    b, l, _ = x.shape
    t = T_PROMPT
    n_chunks = l // t
    cos, sin = _rope_tables(jnp.arange(l))
    intra, q_decay, k_decay, chunk_decay = _decay_tables(_log_gammas(), t)
    qdec = jnp.broadcast_to(q_decay[:, :, None], (N_HEADS, t, HEAD_DIM))
    kdec = jnp.broadcast_to(k_decay[:, :, None], (N_HEADS, t, HEAD_DIM))

    def body(cd_ref, *refs):
        cd = tuple(cd_ref[i] for i in range(N_HEADS))
        _prompt_mixer_kernel(cd, *refs)

    tok = pl.BlockSpec((1, t, D_MODEL), lambda i, j: (i, j, 0))
    rope = pl.BlockSpec((t, HALF), lambda i, j: (j, 0))
    mem = pl.BlockSpec((1, N_MEM, D_MODEL), lambda i, j: (i, 0, 0))
    row = _const_spec((1, D_MODEL))
    table = _const_spec((N_HEADS, t, HEAD_DIM))
    return pl.pallas_call(
        body,
        grid=(b, n_chunks),
        in_specs=[pl.BlockSpec(memory_space=pltpu.SMEM),
                  tok, rope, rope, row, _const_spec((D_MODEL, D_IN)), row,
                  _const_spec((CONV_WIDTH, D_MODEL)), row, row, row,
                  _const_spec((D_MODEL, D_MODEL)), mem, mem, table, table, table],
        out_specs=[tok,
                   pl.BlockSpec((1, N_HEADS, HEAD_DIM, HEAD_DIM), lambda i, j: (i, 0, 0, 0)),
                   pl.BlockSpec((1, HIST, D_MODEL), lambda i, j: (i, 0, 0))],
        out_shape=[jax.ShapeDtypeStruct((b, l, D_MODEL), BF16),
                   jax.ShapeDtypeStruct((b, N_HEADS, HEAD_DIM, HEAD_DIM), F32),
                   jax.ShapeDtypeStruct((b, HIST, D_MODEL), F32)],
        scratch_shapes=[pltpu.VMEM((HIST_PAD + t, D_MODEL), F32)]
        + [pltpu.VMEM((t, D_MODEL), F32)] * 4,
        compiler_params=_params(2),
        name="prompt_mixer",
    )(chunk_decay, x, cos, sin, w["g_mix"], w["w_in"], w["ret_gn_g"], w["conv_w"],
      w["conv_b"], w["conv_ln_g"], w["conv_ln_b"], w["w_conv_out"], kb, vb, intra, qdec, kdec)


def _ffn_kernel(x_ref, m_ref, wout_ref, gffn_ref, wup_ref, wdown_ref, gfin_ref, y_ref):
    groups = [slice(r, r + FFN_ROWS) for r in range(0, T_FFN, FFN_ROWS)]
    x1 = [x_ref[g, :] + _dot(m_ref[g, :].astype(BF16), wout_ref[...]) for g in groups]
    h2 = [_rms(v, gffn_ref[...]).astype(BF16) for v in x1]
    a = [jnp.square(jnp.maximum(_dot(v, wup_ref[...]), 0.0)).astype(BF16) for v in h2]
    for g, v, act in zip(groups, x1, a):
        y_ref[g, :] = _rms(v + _dot(act, wdown_ref[...]), gfin_ref[...])


def _ffn(x, merged, w):
    m = x.shape[0]
    tok = pl.BlockSpec((T_FFN, D_MODEL), lambda i: (i, 0))
    row = _const_spec((1, D_MODEL))
    return pl.pallas_call(
        _ffn_kernel,
        grid=(m // T_FFN,),
        in_specs=[tok, tok, _const_spec((D_MODEL, D_MODEL)), row, _const_spec((D_MODEL, D_FF)),
                  _const_spec((D_FF, D_MODEL)), row],
        out_specs=tok,
        out_shape=jax.ShapeDtypeStruct((m, D_MODEL), F32),
        compiler_params=_params(1),
        name="ffn",
    )(x, merged, w["w_out"], w["g_ffn"], w["w_up"], w["w_down"], w["g_final"])


SAMPLE_PROJ_COLS = 1024


def _sample_proj_kernel(x_ref, g_ref, w_ref, o_ref):
    h = _rms(x_ref[...], g_ref[...]).astype(BF16)
    o_ref[...] = _dot(h, w_ref[...])


def _sample_proj(x, w):
    m = x.shape[0]
    return pl.pallas_call(
        _sample_proj_kernel,
        grid=(D_IN // SAMPLE_PROJ_COLS,),
        in_specs=[_const_spec((m, D_MODEL)), _const_spec((1, D_MODEL)),
                  pl.BlockSpec((D_MODEL, SAMPLE_PROJ_COLS), lambda j: (0, j))],
        out_specs=pl.BlockSpec((m, SAMPLE_PROJ_COLS), lambda j: (0, j)),
        out_shape=jax.ShapeDtypeStruct((m, D_IN), F32),
        compiler_params=_params(1),
        name="sample_proj",
    )(x, w["g_mix"], w["w_in"])


def _head_rows(ref, bi, hd):
    halves = [ref[bi, pl.ds(half * N_HEADS + hd, N_MEM, stride=CACHE_ROWS), :]
              for half in range(HEAD_DIM // LANES)]
    return jnp.concatenate(halves, axis=-1)


def _sample_mixer_kernel(chunk_decay, l,
                         p_ref, cos_ref, sin_ref, s_ref, cbuf_ref, k_ref, v_ref,
                         gn_ref, cw_ref, cb_ref, lng_ref, lnb_ref, wco_ref,
                         dmat_ref, qdec_ref, kdec_ref,
                         merged_ref, sout_ref, cout_ref,
                         ext_scr, y_scr):
    rows = B_SAMPLE * l
    pad = 128
    row_id = lax.broadcasted_iota(jnp.int32, (rows, HEAD_DIM), 0)
    col_id = lax.broadcasted_iota(jnp.int32, (HEAD_DIM, pad), 1)
    row_of = [(row_id >= bi * l) & (row_id < (bi + 1) * l) for bi in range(B_SAMPLE)]
    col_of = [(col_id >= bi * l) & (col_id < (bi + 1) * l) for bi in range(B_SAMPLE)]

    def proj(off, n=HEAD_DIM):
        return p_ref[:, off:off + n]

    u = proj(OFF_GLU_A, D_MODEL) * jax.nn.sigmoid(proj(OFF_GLU_G, D_MODEL))
    for bi in range(B_SAMPLE):
        ext_scr[0:HIST, :] = cbuf_ref[bi]
        ext_scr[HIST:HIST + l, :] = u[bi * l:(bi + 1) * l, :]
        cout_ref[bi] = ext_scr[l:l + HIST, :]
        for i in range(l):
            y_scr[bi * l + i:bi * l + i + 1, :] = (
                jnp.sum(ext_scr[i:i + CONV_WIDTH, :] * cw_ref[...], axis=0, keepdims=True)
                + cb_ref[...])
    act = _silu(_standardize(y_scr[...]) * lng_ref[...] + lnb_ref[...])

    cos, sin = cos_ref[...], sin_ref[...]
    zeros_pad = jnp.zeros((pad - rows, HEAD_DIM), F32)

    def first_stage(hd):
        qb = _rope(proj(OFF_Q + hd * HEAD_DIM), cos, sin).astype(BF16)
        kr = _rope(proj(OFF_K + hd * HEAD_DIM), cos, sin) * QK_SCALE
        v = proj(OFF_V + hd * HEAD_DIM)
        kb_pad = jnp.concatenate([kr, zeros_pad], axis=0).astype(BF16)
        vb_pad = jnp.concatenate([v, zeros_pad], axis=0).astype(BF16)
        scores = _dot_nt(qb, kb_pad) * dmat_ref[hd]
        kd_t = jnp.concatenate([kr * kdec_ref[hd], zeros_pad], axis=0).T
        qx = proj(OFF_XA + hd * HEAD_DIM).astype(BF16)
        o_cross = jnp.zeros((rows, HEAD_DIM), F32)
        s = jnp.zeros((rows, N_MEM), F32)
        for bi in range(B_SAMPLE):
            s_old = s_ref[bi, hd]
            o_cross = jnp.where(row_of[bi], _dot(qb, s_old.astype(BF16)), o_cross)
            kd_bi = jnp.where(col_of[bi], kd_t, 0.0).astype(BF16)
            sout_ref[bi, hd] = s_old * chunk_decay[hd] + _dot(kd_bi, vb_pad)
            s = jnp.where(row_of[bi], _dot_nt(qx, _head_rows(k_ref, bi, hd).astype(BF16)), s)
        return scores, vb_pad, o_cross, s * QK_SCALE

    def second_stage(hd, scores, vb_pad, o_cross, s):
        sl = slice(hd * HEAD_DIM, (hd + 1) * HEAD_DIM)
        o = _dot(scores.astype(BF16), vb_pad) + o_cross * qdec_ref[hd]
        ret = _standardize(o) * gn_ref[:, sl] * _silu(proj(OFF_G + hd * HEAD_DIM))
        e = jnp.exp(s - jnp.max(s, axis=-1, keepdims=True))
        pb = (e / jnp.sum(e, axis=-1, keepdims=True)).astype(BF16)
        xa = jnp.zeros((rows, HEAD_DIM), F32)
        for bi in range(B_SAMPLE):
            xa = jnp.where(row_of[bi], _dot(pb, _head_rows(v_ref, bi, hd).astype(BF16)), xa)
        g0 = jax.nn.sigmoid(proj(OFF_GATE + hd * HEAD_DIM))
        g2 = jax.nn.sigmoid(proj(OFF_GATE + 2 * D_MODEL + hd * HEAD_DIM))
        return g0 * ret + g2 * xa

    parts = []
    staged = first_stage(0)
    for hd in range(N_HEADS):
        following = first_stage(hd + 1) if hd + 1 < N_HEADS else None
        parts.append(second_stage(hd, *staged))
        staged = following
    conv = _dot(act.astype(BF16), wco_ref[...])
    g1 = jax.nn.sigmoid(proj(OFF_GATE + D_MODEL, D_MODEL))
    merged_ref[...] = jnp.concatenate(parts, axis=-1) + g1 * conv


def _sample_mixer(proj, state_ret, state_conv, cache_k, cache_v, w, l):
    nb = state_ret.shape[0]
    rows = B_SAMPLE * l
    pad = 128
    cos, sin = _rope_tables(PAST_LEN + jnp.arange(l))
    cos, sin = jnp.tile(cos, (B_SAMPLE, 1)), jnp.tile(sin, (B_SAMPLE, 1))
    intra, q_decay, k_decay, chunk_decay = _decay_tables(_log_gammas(), l)
    same_req = (jnp.arange(rows)[:, None] // l) == (jnp.arange(pad)[None, :] // l)
    dmat = jnp.where(same_req[None], jnp.pad(jnp.tile(intra, (1, B_SAMPLE, B_SAMPLE)),
                                              ((0, 0), (0, 0), (0, pad - rows))), 0.0)
    qdec = jnp.broadcast_to(jnp.tile(q_decay, (1, B_SAMPLE))[:, :, None], (N_HEADS, rows, HEAD_DIM))
    kdec = jnp.broadcast_to(jnp.tile(k_decay, (1, B_SAMPLE))[:, :, None], (N_HEADS, rows, HEAD_DIM))

    def body(cd_ref, *refs):
        cd = tuple(cd_ref[i] for i in range(N_HEADS))
        _sample_mixer_kernel(cd, l, *refs)

    row = _const_spec((1, D_MODEL))
    state = pl.BlockSpec((B_SAMPLE, N_HEADS, HEAD_DIM, HEAD_DIM), lambda i: (i, 0, 0, 0))
    cbuf = pl.BlockSpec((B_SAMPLE, HIST, D_MODEL), lambda i: (i, 0, 0))
    cache = pl.BlockSpec((B_SAMPLE, N_MEM * CACHE_ROWS, LANES), lambda i: (i, 0, 0))
    return pl.pallas_call(
        body,
        grid=(nb // B_SAMPLE,),
        in_specs=[pl.BlockSpec(memory_space=pltpu.SMEM),
                  pl.BlockSpec((rows, D_IN), lambda i: (i, 0)),
                  _const_spec((rows, HALF)), _const_spec((rows, HALF)),
                  state, cbuf, cache, cache,
                  row, _const_spec((CONV_WIDTH, D_MODEL)), row, row, row,
                  _const_spec((D_MODEL, D_MODEL)),
                  _const_spec((N_HEADS, rows, pad)),
                  _const_spec((N_HEADS, rows, HEAD_DIM)), _const_spec((N_HEADS, rows, HEAD_DIM))],
        out_specs=[pl.BlockSpec((rows, D_MODEL), lambda i: (i, 0)), state, cbuf],
        out_shape=[jax.ShapeDtypeStruct((nb * l, D_MODEL), F32),
                   jax.ShapeDtypeStruct(state_ret.shape, F32),
                   jax.ShapeDtypeStruct(state_conv.shape, F32)],
        scratch_shapes=[pltpu.VMEM((HIST + l + 6, D_MODEL), F32),
                        pltpu.VMEM((rows, D_MODEL), F32)],
        compiler_params=_params(1),
        name="sample_mixer",
    )(chunk_decay, proj, cos, sin, state_ret, state_conv, cache_k, cache_v,
      w["ret_gn_g"], w["conv_w"], w["conv_b"], w["conv_ln_g"], w["conv_ln_b"], w["w_conv_out"],
      dmat, qdec, kdec)


def kernel(x_prompt, x_sample, mem_prompt, state_ret, state_conv, cache_mem_k, cache_mem_v, g_mix, w_in, ret_gn_g, conv_w, conv_b, conv_ln_g, conv_ln_b, w_conv_out, w_out, g_ffn, w_up, w_down, g_mem, w_mem_kv, g_final):
    assert state_ret.shape[0] == 1, "one layer"
    b_p, l_p, _ = x_prompt.shape
    b_s, l_s, _ = x_sample.shape
    w = {
        "g_mix": g_mix[0][None], "w_in": w_in[0].astype(BF16), "ret_gn_g": ret_gn_g[0][None],
        "conv_w": conv_w[0], "conv_b": conv_b[0][None], "conv_ln_g": conv_ln_g[0][None],
        "conv_ln_b": conv_ln_b[0][None], "w_conv_out": w_conv_out[0].astype(BF16),
        "w_out": w_out[0].astype(BF16), "g_ffn": g_ffn[0][None], "w_up": w_up[0].astype(BF16),
        "w_down": w_down[0].astype(BF16), "g_final": g_final[None],
    }

    mk, mv, mkb, mvb = _memkv(mem_prompt, g_mem[0][None], w_mem_kv[0].astype(BF16))
    merged_p, ret_p, conv_p = _prompt_mixer(x_prompt, mkb, mvb, w)
    y_prompt = _ffn(x_prompt.reshape(b_p * l_p, D_MODEL), merged_p.reshape(b_p * l_p, D_MODEL), w)

    xs = x_sample.reshape(b_s * l_s, D_MODEL)
    proj_s = _sample_proj(xs, w)
    merged_s, ret_s, conv_s = _sample_mixer(
        proj_s, state_ret[0], state_conv[0],
        _cache_rows(cache_mem_k[0]), _cache_rows(cache_mem_v[0]),
        w, l_s)
    y_sample = _ffn(xs, merged_s, w)

    return (y_prompt.reshape(b_p, l_p, D_MODEL), y_sample.reshape(b_s, l_s, D_MODEL),
            ret_p[None], conv_p[None], mk[None], mv[None], ret_s[None], conv_s[None])
```

```python
import functools

import jax
import jax.numpy as jnp
from jax import lax
from jax.experimental import pallas as pl
from jax.experimental.pallas import tpu as pltpu

F32 = jnp.float32
BF16 = jnp.bfloat16

D_MODEL = 1024
N_HEADS = 4
HEAD_DIM = 256
HALF = HEAD_DIM // 2
D_FF = 4 * D_MODEL
CONV_WIDTH = 31
HIST = CONV_WIDTH - 1
N_MEM = 256
PAST_LEN = 16384
ROPE_BASE = 10000.0
EPS = 1e-6
QK_SCALE = HEAD_DIM ** -0.5

OFF_Q, OFF_K, OFF_V, OFF_G = 0, 1024, 2048, 3072
OFF_GLU_A, OFF_GLU_G, OFF_XA, OFF_GATE = 4096, 5120, 6144, 7168
D_IN = 10240

SUBLANES = 8
HIST_PAD = 32
T_PROMPT = 256
T_FFN = 512
FFN_ROWS = 256
B_SAMPLE = 4
VMEM_LIMIT = 56 * 1024 * 1024

LANES = 128
CACHE_ROWS = N_HEADS * HEAD_DIM // LANES

NT_DIMS = (((1,), (1,)), ((), ()))


def _cache_rows(c):
    b = c.shape[0]
    c = c.reshape(b, N_MEM, N_HEADS, HEAD_DIM // LANES, LANES).transpose(0, 1, 3, 2, 4)
    return c.reshape(b, N_MEM * CACHE_ROWS, LANES)


def _cache_from_rows(r):
    b = r.shape[0]
    r = r.reshape(b, N_MEM, HEAD_DIM // LANES, N_HEADS, LANES).transpose(0, 1, 3, 2, 4)
    return r.reshape(b, N_MEM, N_HEADS, HEAD_DIM)


def _dot(a, b):
    return jnp.dot(a, b, preferred_element_type=F32)


def _dot_nt(a, b):
    return lax.dot_general(a, b, NT_DIMS, preferred_element_type=F32)


def _rms(x, g):
    return x * lax.rsqrt(jnp.mean(x * x, axis=-1, keepdims=True) + EPS) * g


def _standardize(x):
    mu = jnp.mean(x, axis=-1, keepdims=True)
    xc = x - mu
    return xc * lax.rsqrt(jnp.mean(xc * xc, axis=-1, keepdims=True) + EPS)


def _silu(x):
    return x * jax.nn.sigmoid(x)


def _rope(x, cos, sin):
    x1, x2 = x[:, :HALF], x[:, HALF:]
    return jnp.concatenate([x1 * cos - x2 * sin, x2 * cos + x1 * sin], axis=-1)


def _const_spec(shape):
    return pl.BlockSpec(shape, lambda *_: (0,) * len(shape), pipeline_mode=pl.Buffered(1))


def _params(n_grid):
    return pltpu.CompilerParams(dimension_semantics=("arbitrary",) * n_grid,
                                vmem_limit_bytes=VMEM_LIMIT)


def _memkv_kernel(mem_ref, g_ref, w_ref, k_ref, v_ref, kb_ref, vb_ref, wb_scr):
    @pl.when(pl.program_id(0) == 0)
    def _():
        wb_scr[...] = w_ref[...].astype(BF16)

    h = _rms(mem_ref[0], g_ref[...]).astype(BF16)
    kv = _dot(h, wb_scr[...])
    k, v = kv[:, :D_MODEL], kv[:, D_MODEL:]
    kb_ref[0] = k.astype(BF16)
    vb_ref[0] = v.astype(BF16)
    for hd in range(N_HEADS):
        for half in range(HEAD_DIM // LANES):
            rows = pl.ds(half * N_HEADS + hd, N_MEM, stride=CACHE_ROWS)
            cols = slice(hd * HEAD_DIM + half * LANES, hd * HEAD_DIM + (half + 1) * LANES)
            k_ref[0, rows, :] = k[:, cols]
            v_ref[0, rows, :] = v[:, cols]


def _memkv(mem, g_mem, w_mem_kv):
    b = mem.shape[0]
    blk = pl.BlockSpec((1, N_MEM, D_MODEL), lambda i: (i, 0, 0))
    rows = pl.BlockSpec((1, N_MEM * CACHE_ROWS, LANES), lambda i: (i, 0, 0))
    k, v, kb, vb = pl.pallas_call(
        _memkv_kernel,
        grid=(b,),
        in_specs=[blk, _const_spec((1, D_MODEL)), _const_spec((D_MODEL, 2 * D_MODEL))],
        out_specs=[rows, rows, blk, blk],
        out_shape=[jax.ShapeDtypeStruct((b, N_MEM * CACHE_ROWS, LANES), F32)] * 2
        + [jax.ShapeDtypeStruct((b, N_MEM, D_MODEL), BF16)] * 2,
        scratch_shapes=[pltpu.VMEM((D_MODEL, 2 * D_MODEL), BF16)],
        compiler_params=_params(1),
        name="memkv",
    )(mem, g_mem, w_mem_kv)
    return _cache_from_rows(k), _cache_from_rows(v), kb, vb


CONV_ROWS = 128
CONV_SPLIT = (3, 3, 2)


def _depthwise_conv(ext_ref, cw_ref, cb_ref, y_ref, t, lane_slices):
    first = HIST_PAD - HIST
    for j in lane_slices:
        cols = slice(j * LANES, (j + 1) * LANES)
        for r0 in range(0, t, CONV_ROWS):
            acc = jnp.broadcast_to(cb_ref[:, cols], (CONV_ROWS, LANES))
            for r in range(SUBLANES):
                rows = CONV_ROWS + (SUBLANES if r else 0)
                part = None
                for off in range(r, first + CONV_WIDTH, SUBLANES):
                    if off < first:
                        continue
                    w = off - first
                    base = r0 + off - r
                    term = ext_ref[base:base + rows, cols] * cw_ref[w:w + 1, cols]
                    part = term if part is None else part + term
                acc = acc + part[r:r + CONV_ROWS]
            y_ref[r0:r0 + CONV_ROWS, cols] = acc


def _prompt_mixer_kernel(chunk_decay,
                         x_ref, cos_ref, sin_ref, gmix_ref, win_ref, gn_ref, cw_ref, cb_ref,
                         lng_ref, lnb_ref, wco_ref, kb_ref, vb_ref, dmat_ref, qdec_ref, kdec_ref,
                         merged_ref, s_ref, cout_ref,
                         ext_scr, y_scr, conv_scr, part_scr, g1_scr):
    t = T_PROMPT

    @pl.when(pl.program_id(1) == 0)
    def _():
        s_ref[...] = jnp.zeros_like(s_ref)
        ext_scr[0:HIST_PAD, :] = jnp.zeros((HIST_PAD, D_MODEL), F32)

    h = _rms(x_ref[0], gmix_ref[...]).astype(BF16)

    def proj(off):
        return _dot(h, win_ref[:, off:off + HEAD_DIM])

    for j in range(D_MODEL // HEAD_DIM):
        cols = slice(j * HEAD_DIM, (j + 1) * HEAD_DIM)
        ext_scr[HIST_PAD:HIST_PAD + t, cols] = (
            proj(OFF_GLU_A + j * HEAD_DIM) * jax.nn.sigmoid(proj(OFF_GLU_G + j * HEAD_DIM)))

    cos, sin = cos_ref[...], sin_ref[...]
    assert sum(CONV_SPLIT) == D_MODEL // LANES and len(CONV_SPLIT) < N_HEADS
    for hd in range(N_HEADS):
        sl = slice(hd * HEAD_DIM, (hd + 1) * HEAD_DIM)
        if hd < len(CONV_SPLIT):
            first_slice = sum(CONV_SPLIT[:hd])
            _depthwise_conv(ext_scr, cw_ref, cb_ref, y_scr, t,
                            range(first_slice, first_slice + CONV_SPLIT[hd]))
        elif hd == len(CONV_SPLIT):
            act = _silu(_standardize(y_scr[...]) * lng_ref[...] + lnb_ref[...])
            conv_scr[...] = _dot(act.astype(BF16), wco_ref[...])
            done = slice(0, hd * HEAD_DIM)
            merged_ref[0, :, done] = (
                part_scr[:, done] + g1_scr[:, done] * conv_scr[:, done]).astype(BF16)
        qb = _rope(proj(OFF_Q + hd * HEAD_DIM), cos, sin).astype(BF16)
        kr = _rope(proj(OFF_K + hd * HEAD_DIM), cos, sin) * QK_SCALE
        vb = proj(OFF_V + hd * HEAD_DIM).astype(BF16)
        swish_g = _silu(proj(OFF_G + hd * HEAD_DIM))
        scores = _dot_nt(qb, kr.astype(BF16)) * dmat_ref[hd]
        qx = proj(OFF_XA + hd * HEAD_DIM).astype(BF16)
        s_old = s_ref[0, hd]
        o_cross = _dot(qb, s_old.astype(BF16)) * qdec_ref[hd]
        g0 = jax.nn.sigmoid(proj(OFF_GATE + hd * HEAD_DIM))
        o = _dot(scores.astype(BF16), vb) + o_cross
        s = _dot_nt(qx, kb_ref[0, :, sl]) * QK_SCALE
        g2 = jax.nn.sigmoid(proj(OFF_GATE + 2 * D_MODEL + hd * HEAD_DIM))
        kd_t = (kr * kdec_ref[hd]).T.astype(BF16)
        s_ref[0, hd] = s_old * chunk_decay[hd] + _dot(kd_t, vb)
        g1 = jax.nn.sigmoid(proj(OFF_GATE + D_MODEL + hd * HEAD_DIM))
        e = jnp.exp(s - jnp.max(s, axis=-1, keepdims=True))
        p = e / jnp.sum(e, axis=-1, keepdims=True)
        xa = _dot(p.astype(BF16), vb_ref[0, :, sl])
        ret = _standardize(o) * gn_ref[:, sl] * swish_g
        part = g0 * ret + g2 * xa
        if hd < len(CONV_SPLIT):
            part_scr[:, sl] = part
            g1_scr[:, sl] = g1
        else:
            merged_ref[0, :, sl] = (part + g1 * conv_scr[:, sl]).astype(BF16)

    cout_ref[0] = ext_scr[t + HIST_PAD - HIST:t + HIST_PAD, :]
    ext_scr[0:HIST_PAD, :] = ext_scr[t:t + HIST_PAD, :]


def _decay_tables(log_gamma, c):
    idx = jnp.arange(c, dtype=F32)
    diff = idx[:, None] - idx[None, :]
    causal = diff >= 0
    intra = jnp.where(causal[None],
                      jnp.exp(log_gamma[:, None, None] * jnp.where(causal, diff, 0.0)[None]), 0.0)
    q_decay = jnp.exp(log_gamma[:, None] * (idx + 1.0))
    k_decay = jnp.exp(log_gamma[:, None] * (c - 1.0 - idx))
    chunk_decay = jnp.exp(log_gamma * c)
    return intra, q_decay, k_decay, chunk_decay


def _log_gammas():
    return jnp.log1p(-jnp.exp2(-5.0 - jnp.arange(N_HEADS, dtype=F32)))


def _rope_tables(pos):
    inv = ROPE_BASE ** (-jnp.arange(0, HEAD_DIM, 2, dtype=F32) / HEAD_DIM)
    ang = pos.astype(F32)[:, None] * inv[None, :]
    return jnp.cos(ang), jnp.sin(ang)


def _prompt_mixer(x, kb, vb, w):
    b, l, _ = x.shape
    t = T_PROMPT
    n_chunks = l // t
    cos, sin = _rope_tables(jnp.arange(l))
    intra, q_decay, k_decay, chunk_decay = _decay_tables(_log_gammas(), t)
    qdec = jnp.broadcast_to(q_decay[:, :, None], (N_HEADS, t, HEAD_DIM))
    kdec = jnp.broadcast_to(k_decay[:, :, None], (N_HEADS, t, HEAD_DIM))

    def body(cd_ref, *refs):
        cd = tuple(cd_ref[i] for i in range(N_HEADS))
        _prompt_mixer_kernel(cd, *refs)

    tok = pl.BlockSpec((1, t, D_MODEL), lambda i, j: (i, j, 0))
    rope = pl.BlockSpec((t, HALF), lambda i, j: (j, 0))
    mem = pl.BlockSpec((1, N_MEM, D_MODEL), lambda i, j: (i, 0, 0))
    row = _const_spec((1, D_MODEL))
    table = _const_spec((N_HEADS, t, HEAD_DIM))
    return pl.pallas_call(
        body,
        grid=(b, n_chunks),
        in_specs=[pl.BlockSpec(memory_space=pltpu.SMEM),
                  tok, rope, rope, row, _const_spec((D_MODEL, D_IN)), row,
                  _const_spec((CONV_WIDTH, D_MODEL)), row, row, row,
                  _const_spec((D_MODEL, D_MODEL)), mem, mem, table, table, table],
        out_specs=[tok,
                   pl.BlockSpec((1, N_HEADS, HEAD_DIM, HEAD_DIM), lambda i, j: (i, 0, 0, 0)),
                   pl.BlockSpec((1, HIST, D_MODEL), lambda i, j: (i, 0, 0))],
        out_shape=[jax.ShapeDtypeStruct((b, l, D_MODEL), BF16),
                   jax.ShapeDtypeStruct((b, N_HEADS, HEAD_DIM, HEAD_DIM), F32),
                   jax.ShapeDtypeStruct((b, HIST, D_MODEL), F32)],
        scratch_shapes=[pltpu.VMEM((HIST_PAD + t, D_MODEL), F32)]
        + [pltpu.VMEM((t, D_MODEL), F32)] * 4,
        compiler_params=_params(2),
        name="prompt_mixer",
    )(chunk_decay, x, cos, sin, w["g_mix"], w["w_in"], w["ret_gn_g"], w["conv_w"],
      w["conv_b"], w["conv_ln_g"], w["conv_ln_b"], w["w_conv_out"], kb, vb, intra, qdec, kdec)


def _ffn_kernel(x_ref, m_ref, wout_ref, gffn_ref, wup_ref, wdown_ref, gfin_ref, y_ref):
    groups = [slice(r, r + FFN_ROWS) for r in range(0, T_FFN, FFN_ROWS)]
    x1 = [x_ref[g, :] + _dot(m_ref[g, :].astype(BF16), wout_ref[...]) for g in groups]
    h2 = [_rms(v, gffn_ref[...]).astype(BF16) for v in x1]
    a = [jnp.square(jnp.maximum(_dot(v, wup_ref[...]), 0.0)).astype(BF16) for v in h2]
    for g, v, act in zip(groups, x1, a):
        y_ref[g, :] = _rms(v + _dot(act, wdown_ref[...]), gfin_ref[...])


def _ffn(x, merged, w):
    m = x.shape[0]
    tok = pl.BlockSpec((T_FFN, D_MODEL), lambda i: (i, 0))
    row = _const_spec((1, D_MODEL))
    return pl.pallas_call(
        _ffn_kernel,
        grid=(m // T_FFN,),
        in_specs=[tok, tok, _const_spec((D_MODEL, D_MODEL)), row, _const_spec((D_MODEL, D_FF)),
                  _const_spec((D_FF, D_MODEL)), row],
        out_specs=tok,
        out_shape=jax.ShapeDtypeStruct((m, D_MODEL), F32),
        compiler_params=_params(1),
        name="ffn",
    )(x, merged, w["w_out"], w["g_ffn"], w["w_up"], w["w_down"], w["g_final"])


SAMPLE_PROJ_COLS = 1024


def _sample_proj_kernel(x_ref, g_ref, w_ref, o_ref, wb_ref):
    h = _rms(x_ref[...], g_ref[...]).astype(BF16)
    wb = w_ref[...].astype(BF16)
    wb_ref[...] = wb
    o_ref[...] = _dot(h, wb)


def _sample_proj(x, g_mix, w_in):
    m = x.shape[0]
    w_blk = pl.BlockSpec((D_MODEL, SAMPLE_PROJ_COLS), lambda j: (0, j))
    return pl.pallas_call(
        _sample_proj_kernel,
        grid=(D_IN // SAMPLE_PROJ_COLS,),
        in_specs=[_const_spec((m, D_MODEL)), _const_spec((1, D_MODEL)), w_blk],
        out_specs=[pl.BlockSpec((m, SAMPLE_PROJ_COLS), lambda j: (0, j)), w_blk],
        out_shape=[jax.ShapeDtypeStruct((m, D_IN), F32),
                   jax.ShapeDtypeStruct((D_MODEL, D_IN), BF16)],
        compiler_params=_params(1),
        name="sample_proj",
    )(x, g_mix, w_in)


def _head_rows(ref, bi, hd):
    halves = [ref[bi, pl.ds(half * N_HEADS + hd, N_MEM, stride=CACHE_ROWS), :]
              for half in range(HEAD_DIM // LANES)]
    return jnp.concatenate(halves, axis=-1)


def _sample_mixer_kernel(chunk_decay, l,
                         p_ref, cos_ref, sin_ref, s_ref, cbuf_ref, k_ref, v_ref,
                         gn_ref, cw_ref, cb_ref, lng_ref, lnb_ref, wco_ref,
                         dmat_ref, qdec_ref, kdec_ref,
                         merged_ref, sout_ref, cout_ref,
                         ext_scr, y_scr):
    rows = B_SAMPLE * l
    pad = 128
    row_id = lax.broadcasted_iota(jnp.int32, (rows, HEAD_DIM), 0)
    col_id = lax.broadcasted_iota(jnp.int32, (HEAD_DIM, pad), 1)
    row_of = [(row_id >= bi * l) & (row_id < (bi + 1) * l) for bi in range(B_SAMPLE)]
    col_of = [(col_id >= bi * l) & (col_id < (bi + 1) * l) for bi in range(B_SAMPLE)]

    def proj(off, n=HEAD_DIM):
        return p_ref[:, off:off + n]

    u = proj(OFF_GLU_A, D_MODEL) * jax.nn.sigmoid(proj(OFF_GLU_G, D_MODEL))
    for bi in range(B_SAMPLE):
        ext_scr[0:HIST, :] = cbuf_ref[bi]
        ext_scr[HIST:HIST + l, :] = u[bi * l:(bi + 1) * l, :]
        cout_ref[bi] = ext_scr[l:l + HIST, :]
        for i in range(l):
            y_scr[bi * l + i:bi * l + i + 1, :] = (
                jnp.sum(ext_scr[i:i + CONV_WIDTH, :] * cw_ref[...], axis=0, keepdims=True)
                + cb_ref[...])
    act = _silu(_standardize(y_scr[...]) * lng_ref[...] + lnb_ref[...])

    cos, sin = cos_ref[...], sin_ref[...]
    zeros_pad = jnp.zeros((pad - rows, HEAD_DIM), F32)

    def first_stage(hd):
        qb = _rope(proj(OFF_Q + hd * HEAD_DIM), cos, sin).astype(BF16)
        kr = _rope(proj(OFF_K + hd * HEAD_DIM), cos, sin) * QK_SCALE
        v = proj(OFF_V + hd * HEAD_DIM)
        kb_pad = jnp.concatenate([kr, zeros_pad], axis=0).astype(BF16)
        vb_pad = jnp.concatenate([v, zeros_pad], axis=0).astype(BF16)
        scores = _dot_nt(qb, kb_pad) * dmat_ref[hd]
        kd_t = jnp.concatenate([kr * kdec_ref[hd], zeros_pad], axis=0).T
        qx = proj(OFF_XA + hd * HEAD_DIM).astype(BF16)
        o_cross = jnp.zeros((rows, HEAD_DIM), F32)
        s = jnp.zeros((rows, N_MEM), F32)
        for bi in range(B_SAMPLE):
            s_old = s_ref[bi, hd]
            o_cross = jnp.where(row_of[bi], _dot(qb, s_old.astype(BF16)), o_cross)
            kd_bi = jnp.where(col_of[bi], kd_t, 0.0).astype(BF16)
            sout_ref[bi, hd] = s_old * chunk_decay[hd] + _dot(kd_bi, vb_pad)
            s = jnp.where(row_of[bi], _dot_nt(qx, _head_rows(k_ref, bi, hd).astype(BF16)), s)
        return scores, vb_pad, o_cross, s * QK_SCALE

    def second_stage(hd, scores, vb_pad, o_cross, s):
        sl = slice(hd * HEAD_DIM, (hd + 1) * HEAD_DIM)
        o = _dot(scores.astype(BF16), vb_pad) + o_cross * qdec_ref[hd]
        ret = _standardize(o) * gn_ref[:, sl] * _silu(proj(OFF_G + hd * HEAD_DIM))
        e = jnp.exp(s - jnp.max(s, axis=-1, keepdims=True))
        pb = (e / jnp.sum(e, axis=-1, keepdims=True)).astype(BF16)
        xa = jnp.zeros((rows, HEAD_DIM), F32)
        for bi in range(B_SAMPLE):
            xa = jnp.where(row_of[bi], _dot(pb, _head_rows(v_ref, bi, hd).astype(BF16)), xa)
        g0 = jax.nn.sigmoid(proj(OFF_GATE + hd * HEAD_DIM))
        g2 = jax.nn.sigmoid(proj(OFF_GATE + 2 * D_MODEL + hd * HEAD_DIM))
        return g0 * ret + g2 * xa

    parts = []
    staged = first_stage(0)
    for hd in range(N_HEADS):
        following = first_stage(hd + 1) if hd + 1 < N_HEADS else None
        parts.append(second_stage(hd, *staged))
        staged = following
    conv = _dot(act.astype(BF16), wco_ref[...])
    g1 = jax.nn.sigmoid(proj(OFF_GATE + D_MODEL, D_MODEL))
    merged_ref[...] = jnp.concatenate(parts, axis=-1) + g1 * conv


def _sample_mixer(proj, state_ret, state_conv, cache_k, cache_v, w, l):
    nb = state_ret.shape[0]
    rows = B_SAMPLE * l
    pad = 128
    cos, sin = _rope_tables(PAST_LEN + jnp.arange(l))
    cos, sin = jnp.tile(cos, (B_SAMPLE, 1)), jnp.tile(sin, (B_SAMPLE, 1))
    intra, q_decay, k_decay, chunk_decay = _decay_tables(_log_gammas(), l)
    same_req = (jnp.arange(rows)[:, None] // l) == (jnp.arange(pad)[None, :] // l)
    dmat = jnp.where(same_req[None], jnp.pad(jnp.tile(intra, (1, B_SAMPLE, B_SAMPLE)),
                                              ((0, 0), (0, 0), (0, pad - rows))), 0.0)
    qdec = jnp.broadcast_to(jnp.tile(q_decay, (1, B_SAMPLE))[:, :, None], (N_HEADS, rows, HEAD_DIM))
    kdec = jnp.broadcast_to(jnp.tile(k_decay, (1, B_SAMPLE))[:, :, None], (N_HEADS, rows, HEAD_DIM))

    def body(cd_ref, *refs):
        cd = tuple(cd_ref[i] for i in range(N_HEADS))
        _sample_mixer_kernel(cd, l, *refs)

    row = _const_spec((1, D_MODEL))
    state = pl.BlockSpec((B_SAMPLE, N_HEADS, HEAD_DIM, HEAD_DIM), lambda i: (i, 0, 0, 0))
    cbuf = pl.BlockSpec((B_SAMPLE, HIST, D_MODEL), lambda i: (i, 0, 0))
    cache = pl.BlockSpec((B_SAMPLE, N_MEM * CACHE_ROWS, LANES), lambda i: (i, 0, 0))
    return pl.pallas_call(
        body,
        grid=(nb // B_SAMPLE,),
        in_specs=[pl.BlockSpec(memory_space=pltpu.SMEM),
                  pl.BlockSpec((rows, D_IN), lambda i: (i, 0)),
                  _const_spec((rows, HALF)), _const_spec((rows, HALF)),
                  state, cbuf, cache, cache,
                  row, _const_spec((CONV_WIDTH, D_MODEL)), row, row, row,
                  _const_spec((D_MODEL, D_MODEL)),
                  _const_spec((N_HEADS, rows, pad)),
                  _const_spec((N_HEADS, rows, HEAD_DIM)), _const_spec((N_HEADS, rows, HEAD_DIM))],
        out_specs=[pl.BlockSpec((rows, D_MODEL), lambda i: (i, 0)), state, cbuf],
        out_shape=[jax.ShapeDtypeStruct((nb * l, D_MODEL), F32),
                   jax.ShapeDtypeStruct(state_ret.shape, F32),
                   jax.ShapeDtypeStruct(state_conv.shape, F32)],
        scratch_shapes=[pltpu.VMEM((HIST + l + 6, D_MODEL), F32),
                        pltpu.VMEM((rows, D_MODEL), F32)],
        compiler_params=_params(1),
        name="sample_mixer",
    )(chunk_decay, proj, cos, sin, state_ret, state_conv, cache_k, cache_v,
      w["ret_gn_g"], w["conv_w"], w["conv_b"], w["conv_ln_g"], w["conv_ln_b"], w["w_conv_out"],
      dmat, qdec, kdec)


def kernel(x_prompt, x_sample, mem_prompt, state_ret, state_conv, cache_mem_k, cache_mem_v, g_mix, w_in, ret_gn_g, conv_w, conv_b, conv_ln_g, conv_ln_b, w_conv_out, w_out, g_ffn, w_up, w_down, g_mem, w_mem_kv, g_final):
    assert state_ret.shape[0] == 1, "one layer"
    b_p, l_p, _ = x_prompt.shape
    b_s, l_s, _ = x_sample.shape
    w = {
        "g_mix": g_mix[0][None], "ret_gn_g": ret_gn_g[0][None],
        "conv_w": conv_w[0], "conv_b": conv_b[0][None], "conv_ln_g": conv_ln_g[0][None],
        "conv_ln_b": conv_ln_b[0][None], "w_conv_out": w_conv_out[0].astype(BF16),
        "w_out": w_out[0].astype(BF16), "g_ffn": g_ffn[0][None], "w_up": w_up[0].astype(BF16),
        "w_down": w_down[0].astype(BF16), "g_final": g_final[None],
    }

    xs = x_sample.reshape(b_s * l_s, D_MODEL)
    proj_s, w["w_in"] = _sample_proj(xs, w["g_mix"], w_in[0])

    mk, mv, mkb, mvb = _memkv(mem_prompt, g_mem[0][None], w_mem_kv[0])
    merged_p, ret_p, conv_p = _prompt_mixer(x_prompt, mkb, mvb, w)
    y_prompt = _ffn(x_prompt.reshape(b_p * l_p, D_MODEL), merged_p.reshape(b_p * l_p, D_MODEL), w)

    merged_s, ret_s, conv_s = _sample_mixer(
        proj_s, state_ret[0], state_conv[0],
        _cache_rows(cache_mem_k[0]), _cache_rows(cache_mem_v[0]),
        w, l_s)
    y_sample = _ffn(xs, merged_s, w)

    return (y_prompt.reshape(b_p, l_p, D_MODEL), y_sample.reshape(b_s, l_s, D_MODEL),
            ret_p[None], conv_p[None], mk[None], mv[None], ret_s[None], conv_s[None])
```

```python
import functools

import jax
import jax.numpy as jnp
from jax import lax
from jax.experimental import pallas as pl
from jax.experimental.pallas import tpu as pltpu

F32 = jnp.float32
BF16 = jnp.bfloat16

D_MODEL = 1024
N_HEADS = 4
HEAD_DIM = 256
HALF = HEAD_DIM // 2
D_FF = 4 * D_MODEL
CONV_WIDTH = 31
HIST = CONV_WIDTH - 1
N_MEM = 256
PAST_LEN = 16384
ROPE_BASE = 10000.0
EPS = 1e-6
QK_SCALE = HEAD_DIM ** -0.5

OFF_Q, OFF_K, OFF_V, OFF_G = 0, 1024, 2048, 3072
OFF_GLU_A, OFF_GLU_G, OFF_XA, OFF_GATE = 4096, 5120, 6144, 7168
D_IN = 10240

SUBLANES = 8
HIST_PAD = 32
T_PROMPT = 256
T_FFN = 512
FFN_ROWS = 256
B_SAMPLE = 4
VMEM_LIMIT = 56 * 1024 * 1024

LANES = 128
CACHE_ROWS = N_HEADS * HEAD_DIM // LANES

NT_DIMS = (((1,), (1,)), ((), ()))


def _cache_rows(c):
    b = c.shape[0]
    c = c.reshape(b, N_MEM, N_HEADS, HEAD_DIM // LANES, LANES).transpose(0, 1, 3, 2, 4)
    return c.reshape(b, N_MEM * CACHE_ROWS, LANES)


def _cache_from_rows(r):
    b = r.shape[0]
    r = r.reshape(b, N_MEM, HEAD_DIM // LANES, N_HEADS, LANES).transpose(0, 1, 3, 2, 4)
    return r.reshape(b, N_MEM, N_HEADS, HEAD_DIM)


def _dot(a, b):
    return jnp.dot(a, b, preferred_element_type=F32)


def _dot_nt(a, b):
    return lax.dot_general(a, b, NT_DIMS, preferred_element_type=F32)


def _rms(x, g):
    return x * lax.rsqrt(jnp.mean(x * x, axis=-1, keepdims=True) + EPS) * g


def _standardize(x):
    mu = jnp.mean(x, axis=-1, keepdims=True)
    xc = x - mu
    return xc * lax.rsqrt(jnp.mean(xc * xc, axis=-1, keepdims=True) + EPS)


def _silu(x):
    return x * jax.nn.sigmoid(x)


def _rope(x, cos, sin):
    x1, x2 = x[:, :HALF], x[:, HALF:]
    return jnp.concatenate([x1 * cos - x2 * sin, x2 * cos + x1 * sin], axis=-1)


def _const_spec(shape):
    return pl.BlockSpec(shape, lambda *_: (0,) * len(shape), pipeline_mode=pl.Buffered(1))


def _params(n_grid):
    return pltpu.CompilerParams(dimension_semantics=("arbitrary",) * n_grid,
                                vmem_limit_bytes=VMEM_LIMIT)


def _memkv_kernel(mem_ref, g_ref, w_ref, k_ref, v_ref, kb_ref, vb_ref, wb_scr):
    @pl.when(pl.program_id(0) == 0)
    def _():
        wb_scr[...] = w_ref[...].astype(BF16)

    h = _rms(mem_ref[0], g_ref[...]).astype(BF16)
    kv = _dot(h, wb_scr[...])
    k, v = kv[:, :D_MODEL], kv[:, D_MODEL:]
    kb_ref[0] = k.astype(BF16)
    vb_ref[0] = v.astype(BF16)
    for hd in range(N_HEADS):
        for half in range(HEAD_DIM // LANES):
            rows = pl.ds(half * N_HEADS + hd, N_MEM, stride=CACHE_ROWS)
            cols = slice(hd * HEAD_DIM + half * LANES, hd * HEAD_DIM + (half + 1) * LANES)
            k_ref[0, rows, :] = k[:, cols]
            v_ref[0, rows, :] = v[:, cols]


def _memkv(mem, g_mem, w_mem_kv):
    b = mem.shape[0]
    blk = pl.BlockSpec((1, N_MEM, D_MODEL), lambda i: (i, 0, 0))
    rows = pl.BlockSpec((1, N_MEM * CACHE_ROWS, LANES), lambda i: (i, 0, 0))
    k, v, kb, vb = pl.pallas_call(
        _memkv_kernel,
        grid=(b,),
        in_specs=[blk, _const_spec((1, D_MODEL)), _const_spec((D_MODEL, 2 * D_MODEL))],
        out_specs=[rows, rows, blk, blk],
        out_shape=[jax.ShapeDtypeStruct((b, N_MEM * CACHE_ROWS, LANES), F32)] * 2
        + [jax.ShapeDtypeStruct((b, N_MEM, D_MODEL), BF16)] * 2,
        scratch_shapes=[pltpu.VMEM((D_MODEL, 2 * D_MODEL), BF16)],
        compiler_params=_params(1),
        name="memkv",
    )(mem, g_mem, w_mem_kv)
    return _cache_from_rows(k), _cache_from_rows(v), kb, vb


CONV_ROWS = 128


def _depthwise_conv(ext_ref, cw_ref, cb_ref, y_ref, t, lane_slices):
    first = HIST_PAD - HIST
    for j in lane_slices:
        cols = slice(j * LANES, (j + 1) * LANES)
        for r0 in range(0, t, CONV_ROWS):
            acc = jnp.broadcast_to(cb_ref[:, cols], (CONV_ROWS, LANES))
            for r in range(SUBLANES):
                rows = CONV_ROWS + (SUBLANES if r else 0)
                part = None
                for off in range(r, first + CONV_WIDTH, SUBLANES):
                    if off < first:
                        continue
                    w = off - first
                    base = r0 + off - r
                    term = ext_ref[base:base + rows, cols] * cw_ref[w:w + 1, cols]
                    part = term if part is None else part + term
                acc = acc + part[r:r + CONV_ROWS]
            y_ref[r0:r0 + CONV_ROWS, cols] = acc


def _prompt_mixer_kernel(chunk_decay,
                         x_ref, cos_ref, sin_ref, gmix_ref, win_ref, gn_ref, cw_ref, cb_ref,
                         lng_ref, lnb_ref, wco_ref, kb_ref, vb_ref, dmat_ref, qdec_ref, kdec_ref,
                         merged_ref, s_ref, cout_ref,
                         ext_scr, y_scr, conv_scr, part_scr, g1_scr):
    t = T_PROMPT

    @pl.when(pl.program_id(1) == 0)
    def _():
        s_ref[...] = jnp.zeros_like(s_ref)
        ext_scr[0:HIST_PAD, :] = jnp.zeros((HIST_PAD, D_MODEL), F32)

    h = _rms(x_ref[0], gmix_ref[...]).astype(BF16)

    def proj(off):
        return _dot(h, win_ref[:, off:off + HEAD_DIM])

    for j in range(D_MODEL // HEAD_DIM):
        cols = slice(j * HEAD_DIM, (j + 1) * HEAD_DIM)
        ext_scr[HIST_PAD:HIST_PAD + t, cols] = (
            proj(OFF_GLU_A + j * HEAD_DIM) * jax.nn.sigmoid(proj(OFF_GLU_G + j * HEAD_DIM)))

    cos, sin = cos_ref[...], sin_ref[...]
    conv_slices = D_MODEL // LANES // N_HEADS
    last = N_HEADS - 1
    for hd in range(N_HEADS):
        sl = slice(hd * HEAD_DIM, (hd + 1) * HEAD_DIM)
        _depthwise_conv(ext_scr, cw_ref, cb_ref, y_scr, t,
                        range(hd * conv_slices, (hd + 1) * conv_slices))
        qb = _rope(proj(OFF_Q + hd * HEAD_DIM), cos, sin).astype(BF16)
        kr = _rope(proj(OFF_K + hd * HEAD_DIM), cos, sin) * QK_SCALE
        vb = proj(OFF_V + hd * HEAD_DIM).astype(BF16)
        swish_g = _silu(proj(OFF_G + hd * HEAD_DIM))
        scores = _dot_nt(qb, kr.astype(BF16)) * dmat_ref[hd]
        qx = proj(OFF_XA + hd * HEAD_DIM).astype(BF16)
        s_old = s_ref[0, hd]
        o_cross = _dot(qb, s_old.astype(BF16)) * qdec_ref[hd]
        g0 = jax.nn.sigmoid(proj(OFF_GATE + hd * HEAD_DIM))
        o = _dot(scores.astype(BF16), vb) + o_cross
        s = _dot_nt(qx, kb_ref[0, :, sl]) * QK_SCALE
        g2 = jax.nn.sigmoid(proj(OFF_GATE + 2 * D_MODEL + hd * HEAD_DIM))
        kd_t = (kr * kdec_ref[hd]).T.astype(BF16)
        s_ref[0, hd] = s_old * chunk_decay[hd] + _dot(kd_t, vb)
        g1 = jax.nn.sigmoid(proj(OFF_GATE + D_MODEL + hd * HEAD_DIM))
        if hd == last:
            act = _silu(_standardize(y_scr[...]) * lng_ref[...] + lnb_ref[...])
            conv_scr[...] = _dot(act.astype(BF16), wco_ref[...])
            done = slice(0, hd * HEAD_DIM)
            merged_ref[0, :, done] = (
                part_scr[:, done] + g1_scr[:, done] * conv_scr[:, done]).astype(BF16)
        e = jnp.exp(s - jnp.max(s, axis=-1, keepdims=True))
        p = e / jnp.sum(e, axis=-1, keepdims=True)
        xa = _dot(p.astype(BF16), vb_ref[0, :, sl])
        ret = _standardize(o) * gn_ref[:, sl] * swish_g
        part = g0 * ret + g2 * xa
        if hd < last:
            part_scr[:, sl] = part
            g1_scr[:, sl] = g1
        else:
            merged_ref[0, :, sl] = (part + g1 * conv_scr[:, sl]).astype(BF16)

    cout_ref[0] = ext_scr[t + HIST_PAD - HIST:t + HIST_PAD, :]
    ext_scr[0:HIST_PAD, :] = ext_scr[t:t + HIST_PAD, :]


def _decay_tables(log_gamma, c):
    idx = jnp.arange(c, dtype=F32)
    diff = idx[:, None] - idx[None, :]
    causal = diff >= 0
    intra = jnp.where(causal[None],
                      jnp.exp(log_gamma[:, None, None] * jnp.where(causal, diff, 0.0)[None]), 0.0)
    q_decay = jnp.exp(log_gamma[:, None] * (idx + 1.0))
    k_decay = jnp.exp(log_gamma[:, None] * (c - 1.0 - idx))
    chunk_decay = jnp.exp(log_gamma * c)
    return intra, q_decay, k_decay, chunk_decay


def _log_gammas():
    return jnp.log1p(-jnp.exp2(-5.0 - jnp.arange(N_HEADS, dtype=F32)))


def _rope_tables(pos):
    inv = ROPE_BASE ** (-jnp.arange(0, HEAD_DIM, 2, dtype=F32) / HEAD_DIM)
    ang = pos.astype(F32)[:, None] * inv[None, :]
    return jnp.cos(ang), jnp.sin(ang)


def _prompt_mixer(x, kb, vb, w):
    b, l, _ = x.shape
    t = T_PROMPT
    n_chunks = l // t
    cos, sin = _rope_tables(jnp.arange(l))
    intra, q_decay, k_decay, chunk_decay = _decay_tables(_log_gammas(), t)
    qdec = jnp.broadcast_to(q_decay[:, :, None], (N_HEADS, t, HEAD_DIM))
    kdec = jnp.broadcast_to(k_decay[:, :, None], (N_HEADS, t, HEAD_DIM))

    def body(cd_ref, *refs):
        cd = tuple(cd_ref[i] for i in range(N_HEADS))
        _prompt_mixer_kernel(cd, *refs)

    tok = pl.BlockSpec((1, t, D_MODEL), lambda i, j: (i, j, 0))
    rope = pl.BlockSpec((t, HALF), lambda i, j: (j, 0))
    mem = pl.BlockSpec((1, N_MEM, D_MODEL), lambda i, j: (i, 0, 0))
    row = _const_spec((1, D_MODEL))
    table = _const_spec((N_HEADS, t, HEAD_DIM))
    return pl.pallas_call(
        body,
        grid=(b, n_chunks),
        in_specs=[pl.BlockSpec(memory_space=pltpu.SMEM),
                  tok, rope, rope, row, _const_spec((D_MODEL, D_IN)), row,
                  _const_spec((CONV_WIDTH, D_MODEL)), row, row, row,
                  _const_spec((D_MODEL, D_MODEL)), mem, mem, table, table, table],
        out_specs=[tok,
                   pl.BlockSpec((1, N_HEADS, HEAD_DIM, HEAD_DIM), lambda i, j: (i, 0, 0, 0)),
                   pl.BlockSpec((1, HIST, D_MODEL), lambda i, j: (i, 0, 0))],
        out_shape=[jax.ShapeDtypeStruct((b, l, D_MODEL), BF16),
                   jax.ShapeDtypeStruct((b, N_HEADS, HEAD_DIM, HEAD_DIM), F32),
                   jax.ShapeDtypeStruct((b, HIST, D_MODEL), F32)],
        scratch_shapes=[pltpu.VMEM((HIST_PAD + t, D_MODEL), F32)]
        + [pltpu.VMEM((t, D_MODEL), F32)] * 4,
        compiler_params=_params(2),
        name="prompt_mixer",
    )(chunk_decay, x, cos, sin, w["g_mix"], w["w_in"], w["ret_gn_g"], w["conv_w"],
      w["conv_b"], w["conv_ln_g"], w["conv_ln_b"], w["w_conv_out"], kb, vb, intra, qdec, kdec)


def _ffn_kernel(x_ref, m_ref, wout_ref, gffn_ref, wup_ref, wdown_ref, gfin_ref, y_ref):
    groups = [slice(r, r + FFN_ROWS) for r in range(0, T_FFN, FFN_ROWS)]
    x1 = [x_ref[g, :] + _dot(m_ref[g, :].astype(BF16), wout_ref[...]) for g in groups]
    h2 = [_rms(v, gffn_ref[...]).astype(BF16) for v in x1]
    a = [jnp.square(jnp.maximum(_dot(v, wup_ref[...]), 0.0)).astype(BF16) for v in h2]
    for g, v, act in zip(groups, x1, a):
        y_ref[g, :] = _rms(v + _dot(act, wdown_ref[...]), gfin_ref[...])


def _ffn(x, merged, w):
    m = x.shape[0]
    tok = pl.BlockSpec((T_FFN, D_MODEL), lambda i: (i, 0))
    row = _const_spec((1, D_MODEL))
    return pl.pallas_call(
        _ffn_kernel,
        grid=(m // T_FFN,),
        in_specs=[tok, tok, _const_spec((D_MODEL, D_MODEL)), row, _const_spec((D_MODEL, D_FF)),
                  _const_spec((D_FF, D_MODEL)), row],
        out_specs=tok,
        out_shape=jax.ShapeDtypeStruct((m, D_MODEL), F32),
        compiler_params=_params(1),
        name="ffn",
    )(x, merged, w["w_out"], w["g_ffn"], w["w_up"], w["w_down"], w["g_final"])


SAMPLE_PROJ_COLS = 1024


def _sample_proj_kernel(x_ref, g_ref, w_ref, o_ref, wb_ref):
    h = _rms(x_ref[...], g_ref[...]).astype(BF16)
    wb = w_ref[...].astype(BF16)
    wb_ref[...] = wb
    o_ref[...] = _dot(h, wb)


def _sample_proj(x, g_mix, w_in):
    m = x.shape[0]
    w_blk = pl.BlockSpec((D_MODEL, SAMPLE_PROJ_COLS), lambda j: (0, j))
    return pl.pallas_call(
        _sample_proj_kernel,
        grid=(D_IN // SAMPLE_PROJ_COLS,),
        in_specs=[_const_spec((m, D_MODEL)), _const_spec((1, D_MODEL)), w_blk],
        out_specs=[pl.BlockSpec((m, SAMPLE_PROJ_COLS), lambda j: (0, j)), w_blk],
        out_shape=[jax.ShapeDtypeStruct((m, D_IN), F32),
                   jax.ShapeDtypeStruct((D_MODEL, D_IN), BF16)],
        compiler_params=_params(1),
        name="sample_proj",
    )(x, g_mix, w_in)


def _head_rows(ref, bi, hd):
    halves = [ref[bi, pl.ds(half * N_HEADS + hd, N_MEM, stride=CACHE_ROWS), :]
              for half in range(HEAD_DIM // LANES)]
    return jnp.concatenate(halves, axis=-1)


def _sample_mixer_kernel(chunk_decay, l,
                         p_ref, cos_ref, sin_ref, s_ref, cbuf_ref, k_ref, v_ref,
                         gn_ref, cw_ref, cb_ref, lng_ref, lnb_ref, wco_ref,
                         dmat_ref, qdec_ref, kdec_ref,
                         merged_ref, sout_ref, cout_ref,
                         ext_scr, y_scr):
    rows = B_SAMPLE * l
    pad = 128
    row_id = lax.broadcasted_iota(jnp.int32, (rows, HEAD_DIM), 0)
    col_id = lax.broadcasted_iota(jnp.int32, (HEAD_DIM, pad), 1)
    row_of = [(row_id >= bi * l) & (row_id < (bi + 1) * l) for bi in range(B_SAMPLE)]
    col_of = [(col_id >= bi * l) & (col_id < (bi + 1) * l) for bi in range(B_SAMPLE)]

    def proj(off, n=HEAD_DIM):
        return p_ref[:, off:off + n]

    u = proj(OFF_GLU_A, D_MODEL) * jax.nn.sigmoid(proj(OFF_GLU_G, D_MODEL))
    for bi in range(B_SAMPLE):
        ext_scr[0:HIST, :] = cbuf_ref[bi]
        ext_scr[HIST:HIST + l, :] = u[bi * l:(bi + 1) * l, :]
        cout_ref[bi] = ext_scr[l:l + HIST, :]
        for i in range(l):
            y_scr[bi * l + i:bi * l + i + 1, :] = (
                jnp.sum(ext_scr[i:i + CONV_WIDTH, :] * cw_ref[...], axis=0, keepdims=True)
                + cb_ref[...])
    act = _silu(_standardize(y_scr[...]) * lng_ref[...] + lnb_ref[...])

    cos, sin = cos_ref[...], sin_ref[...]
    zeros_pad = jnp.zeros((pad - rows, HEAD_DIM), F32)

    def first_stage(hd):
        qb = _rope(proj(OFF_Q + hd * HEAD_DIM), cos, sin).astype(BF16)
        kr = _rope(proj(OFF_K + hd * HEAD_DIM), cos, sin) * QK_SCALE
        v = proj(OFF_V + hd * HEAD_DIM)
        kb_pad = jnp.concatenate([kr, zeros_pad], axis=0).astype(BF16)
        vb_pad = jnp.concatenate([v, zeros_pad], axis=0).astype(BF16)
        scores = _dot_nt(qb, kb_pad) * dmat_ref[hd]
        kd_t = jnp.concatenate([kr * kdec_ref[hd], zeros_pad], axis=0).T
        qx = proj(OFF_XA + hd * HEAD_DIM).astype(BF16)
        o_cross = jnp.zeros((rows, HEAD_DIM), F32)
        s = jnp.zeros((rows, N_MEM), F32)
        for bi in range(B_SAMPLE):
            s_old = s_ref[bi, hd]
            o_cross = jnp.where(row_of[bi], _dot(qb, s_old.astype(BF16)), o_cross)
            kd_bi = jnp.where(col_of[bi], kd_t, 0.0).astype(BF16)
            sout_ref[bi, hd] = s_old * chunk_decay[hd] + _dot(kd_bi, vb_pad)
            s = jnp.where(row_of[bi], _dot_nt(qx, _head_rows(k_ref, bi, hd).astype(BF16)), s)
        return scores, vb_pad, o_cross, s * QK_SCALE

    def second_stage(hd, scores, vb_pad, o_cross, s):
        sl = slice(hd * HEAD_DIM, (hd + 1) * HEAD_DIM)
        o = _dot(scores.astype(BF16), vb_pad) + o_cross * qdec_ref[hd]
        ret = _standardize(o) * gn_ref[:, sl] * _silu(proj(OFF_G + hd * HEAD_DIM))
        e = jnp.exp(s - jnp.max(s, axis=-1, keepdims=True))
        pb = (e / jnp.sum(e, axis=-1, keepdims=True)).astype(BF16)
        xa = jnp.zeros((rows, HEAD_DIM), F32)
        for bi in range(B_SAMPLE):
            xa = jnp.where(row_of[bi], _dot(pb, _head_rows(v_ref, bi, hd).astype(BF16)), xa)
        g0 = jax.nn.sigmoid(proj(OFF_GATE + hd * HEAD_DIM))
        g2 = jax.nn.sigmoid(proj(OFF_GATE + 2 * D_MODEL + hd * HEAD_DIM))
        return g0 * ret + g2 * xa

    parts = []
    staged = first_stage(0)
    for hd in range(N_HEADS):
        following = first_stage(hd + 1) if hd + 1 < N_HEADS else None
        parts.append(second_stage(hd, *staged))
        staged = following
    conv = _dot(act.astype(BF16), wco_ref[...])
    g1 = jax.nn.sigmoid(proj(OFF_GATE + D_MODEL, D_MODEL))
    merged_ref[...] = jnp.concatenate(parts, axis=-1) + g1 * conv


def _sample_mixer(proj, state_ret, state_conv, cache_k, cache_v, w, l):
    nb = state_ret.shape[0]
    rows = B_SAMPLE * l
    pad = 128
    cos, sin = _rope_tables(PAST_LEN + jnp.arange(l))
    cos, sin = jnp.tile(cos, (B_SAMPLE, 1)), jnp.tile(sin, (B_SAMPLE, 1))
    intra, q_decay, k_decay, chunk_decay = _decay_tables(_log_gammas(), l)
    same_req = (jnp.arange(rows)[:, None] // l) == (jnp.arange(pad)[None, :] // l)
    dmat = jnp.where(same_req[None], jnp.pad(jnp.tile(intra, (1, B_SAMPLE, B_SAMPLE)),
                                              ((0, 0), (0, 0), (0, pad - rows))), 0.0)
    qdec = jnp.broadcast_to(jnp.tile(q_decay, (1, B_SAMPLE))[:, :, None], (N_HEADS, rows, HEAD_DIM))
    kdec = jnp.broadcast_to(jnp.tile(k_decay, (1, B_SAMPLE))[:, :, None], (N_HEADS, rows, HEAD_DIM))

    def body(cd_ref, *refs):
        cd = tuple(cd_ref[i] for i in range(N_HEADS))
        _sample_mixer_kernel(cd, l, *refs)

    row = _const_spec((1, D_MODEL))
    state = pl.BlockSpec((B_SAMPLE, N_HEADS, HEAD_DIM, HEAD_DIM), lambda i: (i, 0, 0, 0))
    cbuf = pl.BlockSpec((B_SAMPLE, HIST, D_MODEL), lambda i: (i, 0, 0))
    cache = pl.BlockSpec((B_SAMPLE, N_MEM * CACHE_ROWS, LANES), lambda i: (i, 0, 0))
    return pl.pallas_call(
        body,
        grid=(nb // B_SAMPLE,),
        in_specs=[pl.BlockSpec(memory_space=pltpu.SMEM),
                  pl.BlockSpec((rows, D_IN), lambda i: (i, 0)),
                  _const_spec((rows, HALF)), _const_spec((rows, HALF)),
                  state, cbuf, cache, cache,
                  row, _const_spec((CONV_WIDTH, D_MODEL)), row, row, row,
                  _const_spec((D_MODEL, D_MODEL)),
                  _const_spec((N_HEADS, rows, pad)),
                  _const_spec((N_HEADS, rows, HEAD_DIM)), _const_spec((N_HEADS, rows, HEAD_DIM))],
        out_specs=[pl.BlockSpec((rows, D_MODEL), lambda i: (i, 0)), state, cbuf],
        out_shape=[jax.ShapeDtypeStruct((nb * l, D_MODEL), F32),
                   jax.ShapeDtypeStruct(state_ret.shape, F32),
                   jax.ShapeDtypeStruct(state_conv.shape, F32)],
        scratch_shapes=[pltpu.VMEM((HIST + l + 6, D_MODEL), F32),
                        pltpu.VMEM((rows, D_MODEL), F32)],
        compiler_params=_params(1),
        name="sample_mixer",
    )(chunk_decay, proj, cos, sin, state_ret, state_conv, cache_k, cache_v,
      w["ret_gn_g"], w["conv_w"], w["conv_b"], w["conv_ln_g"], w["conv_ln_b"], w["w_conv_out"],
      dmat, qdec, kdec)


def kernel(x_prompt, x_sample, mem_prompt, state_ret, state_conv, cache_mem_k, cache_mem_v, g_mix, w_in, ret_gn_g, conv_w, conv_b, conv_ln_g, conv_ln_b, w_conv_out, w_out, g_ffn, w_up, w_down, g_mem, w_mem_kv, g_final):
    assert state_ret.shape[0] == 1, "one layer"
    b_p, l_p, _ = x_prompt.shape
    b_s, l_s, _ = x_sample.shape
    w = {
        "g_mix": g_mix[0][None], "ret_gn_g": ret_gn_g[0][None],
        "conv_w": conv_w[0], "conv_b": conv_b[0][None], "conv_ln_g": conv_ln_g[0][None],
        "conv_ln_b": conv_ln_b[0][None], "w_conv_out": w_conv_out[0].astype(BF16),
        "w_out": w_out[0].astype(BF16), "g_ffn": g_ffn[0][None], "w_up": w_up[0].astype(BF16),
        "w_down": w_down[0].astype(BF16), "g_final": g_final[None],
    }

    xs = x_sample.reshape(b_s * l_s, D_MODEL)
    proj_s, w["w_in"] = _sample_proj(xs, w["g_mix"], w_in[0])

    mk, mv, mkb, mvb = _memkv(mem_prompt, g_mem[0][None], w_mem_kv[0])
    merged_p, ret_p, conv_p = _prompt_mixer(x_prompt, mkb, mvb, w)
    y_prompt = _ffn(x_prompt.reshape(b_p * l_p, D_MODEL), merged_p.reshape(b_p * l_p, D_MODEL), w)

    merged_s, ret_s, conv_s = _sample_mixer(
        proj_s, state_ret[0], state_conv[0],
        _cache_rows(cache_mem_k[0]), _cache_rows(cache_mem_v[0]),
        w, l_s)
    y_sample = _ffn(xs, merged_s, w)

    return (y_prompt.reshape(b_p, l_p, D_MODEL), y_sample.reshape(b_s, l_s, D_MODEL),
            ret_p[None], conv_p[None], mk[None], mv[None], ret_s[None], conv_s[None])
```

```python
import functools

import jax
import jax.numpy as jnp
from jax import lax
from jax.experimental import pallas as pl
from jax.experimental.pallas import tpu as pltpu

F32 = jnp.float32
BF16 = jnp.bfloat16

D_MODEL = 1024
N_HEADS = 4
HEAD_DIM = 256
HALF = HEAD_DIM // 2
D_FF = 4 * D_MODEL
CONV_WIDTH = 31
HIST = CONV_WIDTH - 1
N_MEM = 256
PAST_LEN = 16384
ROPE_BASE = 10000.0
EPS = 1e-6
QK_SCALE = HEAD_DIM ** -0.5

OFF_Q, OFF_K, OFF_V, OFF_G = 0, 1024, 2048, 3072
OFF_GLU_A, OFF_GLU_G, OFF_XA, OFF_GATE = 4096, 5120, 6144, 7168
D_IN = 10240

SUBLANES = 8
HIST_PAD = 32
T_PROMPT = 256
T_FFN = 512
FFN_ROWS = 256
B_SAMPLE = 4
VMEM_LIMIT = 56 * 1024 * 1024

LANES = 128
CACHE_ROWS = N_HEADS * HEAD_DIM // LANES

NT_DIMS = (((1,), (1,)), ((), ()))


def _cache_rows(c):
    b = c.shape[0]
    c = c.reshape(b, N_MEM, N_HEADS, HEAD_DIM // LANES, LANES).transpose(0, 1, 3, 2, 4)
    return c.reshape(b, N_MEM * CACHE_ROWS, LANES)


def _column_slabs(w):
    k, n = w.shape
    return w.reshape(k, n // HEAD_DIM, HEAD_DIM).transpose(1, 0, 2)


def _cache_from_rows(r):
    b = r.shape[0]
    r = r.reshape(b, N_MEM, HEAD_DIM // LANES, N_HEADS, LANES).transpose(0, 1, 3, 2, 4)
    return r.reshape(b, N_MEM, N_HEADS, HEAD_DIM)


def _dot(a, b):
    return jnp.dot(a, b, preferred_element_type=F32)


def _dot_nt(a, b):
    return lax.dot_general(a, b, NT_DIMS, preferred_element_type=F32)


def _rms(x, g):
    return x * lax.rsqrt(jnp.mean(x * x, axis=-1, keepdims=True) + EPS) * g


def _standardize(x):
    mu = jnp.mean(x, axis=-1, keepdims=True)
    xc = x - mu
    return xc * lax.rsqrt(jnp.mean(xc * xc, axis=-1, keepdims=True) + EPS)


def _silu(x):
    return x * jax.nn.sigmoid(x)


def _rope(x, cos, sin):
    x1, x2 = x[:, :HALF], x[:, HALF:]
    return jnp.concatenate([x1 * cos - x2 * sin, x2 * cos + x1 * sin], axis=-1)


def _const_spec(shape):
    return pl.BlockSpec(shape, lambda *_: (0,) * len(shape), pipeline_mode=pl.Buffered(1))


def _params(n_grid):
    return pltpu.CompilerParams(dimension_semantics=("arbitrary",) * n_grid,
                                vmem_limit_bytes=VMEM_LIMIT)


def _memkv_kernel(mem_ref, g_ref, w_ref, k_ref, v_ref, kb_ref, vb_ref, wb_scr):
    @pl.when(pl.program_id(0) == 0)
    def _():
        wb_scr[...] = w_ref[...].astype(BF16)

    h = _rms(mem_ref[0], g_ref[...]).astype(BF16)
    kv = _dot(h, wb_scr[...])
    k, v = kv[:, :D_MODEL], kv[:, D_MODEL:]
    for hd in range(N_HEADS):
        kb_ref[0, hd] = k[:, hd * HEAD_DIM:(hd + 1) * HEAD_DIM].astype(BF16)
        vb_ref[0, hd] = v[:, hd * HEAD_DIM:(hd + 1) * HEAD_DIM].astype(BF16)
    for hd in range(N_HEADS):
        for half in range(HEAD_DIM // LANES):
            rows = pl.ds(half * N_HEADS + hd, N_MEM, stride=CACHE_ROWS)
            cols = slice(hd * HEAD_DIM + half * LANES, hd * HEAD_DIM + (half + 1) * LANES)
            k_ref[0, rows, :] = k[:, cols]
            v_ref[0, rows, :] = v[:, cols]


def _memkv(mem, g_mem, w_mem_kv):
    b = mem.shape[0]
    blk = pl.BlockSpec((1, N_MEM, D_MODEL), lambda i: (i, 0, 0))
    rows = pl.BlockSpec((1, N_MEM * CACHE_ROWS, LANES), lambda i: (i, 0, 0))
    slabs = pl.BlockSpec((1, N_HEADS, N_MEM, HEAD_DIM), lambda i: (i, 0, 0, 0))
    k, v, kb, vb = pl.pallas_call(
        _memkv_kernel,
        grid=(b,),
        in_specs=[blk, _const_spec((1, D_MODEL)), _const_spec((D_MODEL, 2 * D_MODEL))],
        out_specs=[rows, rows, slabs, slabs],
        out_shape=[jax.ShapeDtypeStruct((b, N_MEM * CACHE_ROWS, LANES), F32)] * 2
        + [jax.ShapeDtypeStruct((b, N_HEADS, N_MEM, HEAD_DIM), BF16)] * 2,
        scratch_shapes=[pltpu.VMEM((D_MODEL, 2 * D_MODEL), BF16)],
        compiler_params=_params(1),
        name="memkv",
    )(mem, g_mem, w_mem_kv)
    return _cache_from_rows(k), _cache_from_rows(v), kb, vb


CONV_ROWS = 128


def _depthwise_conv(ext_ref, cw_ref, cb_ref, y_ref, t, lane_slices):
    first = HIST_PAD - HIST
    for j in lane_slices:
        cols = slice(j * LANES, (j + 1) * LANES)
        for r0 in range(0, t, CONV_ROWS):
            acc = jnp.broadcast_to(cb_ref[:, cols], (CONV_ROWS, LANES))
            for r in range(SUBLANES):
                rows = CONV_ROWS + (SUBLANES if r else 0)
                part = None
                for off in range(r, first + CONV_WIDTH, SUBLANES):
                    if off < first:
                        continue
                    w = off - first
                    base = r0 + off - r
                    term = ext_ref[base:base + rows, cols] * cw_ref[w:w + 1, cols]
                    part = term if part is None else part + term
                acc = acc + part[r:r + CONV_ROWS]
            y_ref[r0:r0 + CONV_ROWS, cols] = acc


def _prompt_mixer_kernel(chunk_decay,
                         x_ref, cos_ref, sin_ref, gmix_ref, win_ref, gn_ref, cw_ref, cb_ref,
                         lng_ref, lnb_ref, wco_ref, kb_ref, vb_ref, dmat_ref, qdec_ref, kdec_ref,
                         merged_ref, s_ref, cout_ref,
                         ext_scr, y_scr, conv_scr, part_scr, g1_scr):
    t = T_PROMPT

    @pl.when(pl.program_id(1) == 0)
    def _():
        s_ref[...] = jnp.zeros_like(s_ref)
        ext_scr[0:HIST_PAD, :] = jnp.zeros((HIST_PAD, D_MODEL), F32)

    h = _rms(x_ref[0], gmix_ref[...]).astype(BF16)

    def proj(off):
        return _dot(h, win_ref[off // HEAD_DIM])

    for j in range(D_MODEL // HEAD_DIM):
        cols = slice(j * HEAD_DIM, (j + 1) * HEAD_DIM)
        ext_scr[HIST_PAD:HIST_PAD + t, cols] = (
            proj(OFF_GLU_A + j * HEAD_DIM) * jax.nn.sigmoid(proj(OFF_GLU_G + j * HEAD_DIM)))

    cos, sin = cos_ref[...], sin_ref[...]
    conv_slices = D_MODEL // LANES // N_HEADS
    last = N_HEADS - 1
    for hd in range(N_HEADS):
        sl = slice(hd * HEAD_DIM, (hd + 1) * HEAD_DIM)
        _depthwise_conv(ext_scr, cw_ref, cb_ref, y_scr, t,
                        range(hd * conv_slices, (hd + 1) * conv_slices))
        qb = _rope(proj(OFF_Q + hd * HEAD_DIM), cos, sin).astype(BF16)
        kr = _rope(proj(OFF_K + hd * HEAD_DIM), cos, sin) * QK_SCALE
        vb = proj(OFF_V + hd * HEAD_DIM).astype(BF16)
        swish_g = _silu(proj(OFF_G + hd * HEAD_DIM))
        scores = _dot_nt(qb, kr.astype(BF16)) * dmat_ref[hd]
        qx = proj(OFF_XA + hd * HEAD_DIM).astype(BF16)
        s_old = s_ref[0, hd]
        o_cross = _dot(qb, s_old.astype(BF16)) * qdec_ref[hd]
        g0 = jax.nn.sigmoid(proj(OFF_GATE + hd * HEAD_DIM))
        o = _dot(scores.astype(BF16), vb) + o_cross
        s = _dot_nt(qx, kb_ref[0, hd]) * QK_SCALE
        g2 = jax.nn.sigmoid(proj(OFF_GATE + 2 * D_MODEL + hd * HEAD_DIM))
        kd_t = (kr * kdec_ref[hd]).T.astype(BF16)
        s_ref[0, hd] = s_old * chunk_decay[hd] + _dot(kd_t, vb)
        g1 = jax.nn.sigmoid(proj(OFF_GATE + D_MODEL + hd * HEAD_DIM))
        if hd == last:
            act = _silu(_standardize(y_scr[...]) * lng_ref[...] + lnb_ref[...])
            act = act.astype(BF16)
            for n in range(D_MODEL // HEAD_DIM):
                conv_scr[:, n * HEAD_DIM:(n + 1) * HEAD_DIM] = _dot(act, wco_ref[n])
            done = slice(0, hd * HEAD_DIM)
            merged_ref[0, :, done] = (
                part_scr[:, done] + g1_scr[:, done] * conv_scr[:, done]).astype(BF16)
        e = jnp.exp(s - jnp.max(s, axis=-1, keepdims=True))
        p = e / jnp.sum(e, axis=-1, keepdims=True)
        xa = _dot(p.astype(BF16), vb_ref[0, hd])
        ret = _standardize(o) * gn_ref[:, sl] * swish_g
        part = g0 * ret + g2 * xa
        if hd < last:
            part_scr[:, sl] = part
            g1_scr[:, sl] = g1
        else:
            merged_ref[0, :, sl] = (part + g1 * conv_scr[:, sl]).astype(BF16)

    cout_ref[0] = ext_scr[t + HIST_PAD - HIST:t + HIST_PAD, :]
    ext_scr[0:HIST_PAD, :] = ext_scr[t:t + HIST_PAD, :]


def _decay_tables(log_gamma, c):
    idx = jnp.arange(c, dtype=F32)
    diff = idx[:, None] - idx[None, :]
    causal = diff >= 0
    intra = jnp.where(causal[None],
                      jnp.exp(log_gamma[:, None, None] * jnp.where(causal, diff, 0.0)[None]), 0.0)
    q_decay = jnp.exp(log_gamma[:, None] * (idx + 1.0))
    k_decay = jnp.exp(log_gamma[:, None] * (c - 1.0 - idx))
    chunk_decay = jnp.exp(log_gamma * c)
    return intra, q_decay, k_decay, chunk_decay


def _log_gammas():
    return jnp.log1p(-jnp.exp2(-5.0 - jnp.arange(N_HEADS, dtype=F32)))


def _rope_tables(pos):
    inv = ROPE_BASE ** (-jnp.arange(0, HEAD_DIM, 2, dtype=F32) / HEAD_DIM)
    ang = pos.astype(F32)[:, None] * inv[None, :]
    return jnp.cos(ang), jnp.sin(ang)


def _prompt_mixer(x, kb, vb, w):
    b, l, _ = x.shape
    t = T_PROMPT
    n_chunks = l // t
    cos, sin = _rope_tables(jnp.arange(l))
    intra, q_decay, k_decay, chunk_decay = _decay_tables(_log_gammas(), t)
    qdec = jnp.broadcast_to(q_decay[:, :, None], (N_HEADS, t, HEAD_DIM))
    kdec = jnp.broadcast_to(k_decay[:, :, None], (N_HEADS, t, HEAD_DIM))

    def body(cd_ref, *refs):
        cd = tuple(cd_ref[i] for i in range(N_HEADS))
        _prompt_mixer_kernel(cd, *refs)

    tok = pl.BlockSpec((1, t, D_MODEL), lambda i, j: (i, j, 0))
    rope = pl.BlockSpec((t, HALF), lambda i, j: (j, 0))
    mem = pl.BlockSpec((1, N_HEADS, N_MEM, HEAD_DIM), lambda i, j: (i, 0, 0, 0))
    row = _const_spec((1, D_MODEL))
    table = _const_spec((N_HEADS, t, HEAD_DIM))
    return pl.pallas_call(
        body,
        grid=(b, n_chunks),
        in_specs=[pl.BlockSpec(memory_space=pltpu.SMEM),
                  tok, rope, rope, row, _const_spec((D_IN // HEAD_DIM, D_MODEL, HEAD_DIM)), row,
                  _const_spec((CONV_WIDTH, D_MODEL)), row, row, row,
                  _const_spec((D_MODEL // HEAD_DIM, D_MODEL, HEAD_DIM)), mem, mem,
                  table, table, table],
        out_specs=[tok,
                   pl.BlockSpec((1, N_HEADS, HEAD_DIM, HEAD_DIM), lambda i, j: (i, 0, 0, 0)),
                   pl.BlockSpec((1, HIST, D_MODEL), lambda i, j: (i, 0, 0))],
        out_shape=[jax.ShapeDtypeStruct((b, l, D_MODEL), BF16),
                   jax.ShapeDtypeStruct((b, N_HEADS, HEAD_DIM, HEAD_DIM), F32),
                   jax.ShapeDtypeStruct((b, HIST, D_MODEL), F32)],
        scratch_shapes=[pltpu.VMEM((HIST_PAD + t, D_MODEL), F32)]
        + [pltpu.VMEM((t, D_MODEL), F32)] * 4,
        compiler_params=_params(2),
        name="prompt_mixer",
    )(chunk_decay, x, cos, sin, w["g_mix"], w["w_in"], w["ret_gn_g"], w["conv_w"],
      w["conv_b"], w["conv_ln_g"], w["conv_ln_b"], w["w_conv_out"], kb, vb, intra, qdec, kdec)


def _ffn_kernel(x_ref, m_ref, wout_ref, gffn_ref, wup_ref, wdown_ref, gfin_ref, y_ref):
    groups = [slice(r, r + FFN_ROWS) for r in range(0, T_FFN, FFN_ROWS)]
    x1 = [x_ref[g, :] + _dot(m_ref[g, :].astype(BF16), wout_ref[...]) for g in groups]
    h2 = [_rms(v, gffn_ref[...]).astype(BF16) for v in x1]
    a = [jnp.square(jnp.maximum(_dot(v, wup_ref[...]), 0.0)).astype(BF16) for v in h2]
    for g, v, act in zip(groups, x1, a):
        y_ref[g, :] = _rms(v + _dot(act, wdown_ref[...]), gfin_ref[...])


def _ffn(x, merged, w):
    m = x.shape[0]
    tok = pl.BlockSpec((T_FFN, D_MODEL), lambda i: (i, 0))
    row = _const_spec((1, D_MODEL))
    return pl.pallas_call(
        _ffn_kernel,
        grid=(m // T_FFN,),
        in_specs=[tok, tok, _const_spec((D_MODEL, D_MODEL)), row, _const_spec((D_MODEL, D_FF)),
                  _const_spec((D_FF, D_MODEL)), row],
        out_specs=tok,
        out_shape=jax.ShapeDtypeStruct((m, D_MODEL), F32),
        compiler_params=_params(1),
        name="ffn",
    )(x, merged, w["w_out"], w["g_ffn"], w["w_up"], w["w_down"], w["g_final"])


FF_BLOCK = 512


def _ffn_stream_kernel(x_ref, m_ref, wout_ref, gffn_ref, wup_ref, wdown_ref, gfin_ref,
                       y_ref, wout_b_ref, wup_b_ref, wdown_b_ref,
                       x1_scr, h2_scr, acc_scr):
    j = pl.program_id(0)

    @pl.when(j == 0)
    def _():
        wout = wout_ref[...].astype(BF16)
        wout_b_ref[...] = wout
        x1 = x_ref[...] + _dot(m_ref[...].astype(BF16), wout)
        x1_scr[...] = x1
        h2_scr[...] = _rms(x1, gffn_ref[...]).astype(BF16)
        acc_scr[...] = jnp.zeros_like(acc_scr)

    wup = wup_ref[...].astype(BF16)
    wdown = wdown_ref[...].astype(BF16)
    wup_b_ref[...] = wup
    wdown_b_ref[...] = wdown
    a = jnp.square(jnp.maximum(_dot(h2_scr[...], wup), 0.0)).astype(BF16)
    acc_scr[...] += _dot(a, wdown)

    @pl.when(j == pl.num_programs(0) - 1)
    def _():
        y_ref[...] = _rms(x1_scr[...] + acc_scr[...], gfin_ref[...])


def _ffn_stream(x, merged, w_out, g_ffn, w_up, w_down, g_final):
    m = x.shape[0]
    full = _const_spec((m, D_MODEL))
    row = _const_spec((1, D_MODEL))
    sq = _const_spec((D_MODEL, D_MODEL))
    up = pl.BlockSpec((D_MODEL, FF_BLOCK), lambda j: (0, j))
    down = pl.BlockSpec((FF_BLOCK, D_MODEL), lambda j: (j, 0))
    return pl.pallas_call(
        _ffn_stream_kernel,
        grid=(D_FF // FF_BLOCK,),
        in_specs=[full, full, sq, row, up, down, row],
        out_specs=[pl.BlockSpec((m, D_MODEL), lambda j: (0, 0)),
                   pl.BlockSpec((D_MODEL, D_MODEL), lambda j: (0, 0)), up, down],
        out_shape=[jax.ShapeDtypeStruct((m, D_MODEL), F32),
                   jax.ShapeDtypeStruct((D_MODEL, D_MODEL), BF16),
                   jax.ShapeDtypeStruct((D_MODEL, D_FF), BF16),
                   jax.ShapeDtypeStruct((D_FF, D_MODEL), BF16)],
        scratch_shapes=[pltpu.VMEM((m, D_MODEL), F32), pltpu.VMEM((m, D_MODEL), BF16),
                        pltpu.VMEM((m, D_MODEL), F32)],
        compiler_params=_params(1),
        name="ffn_stream",
    )(x, merged, w_out, g_ffn, w_up, w_down, g_final)


SAMPLE_PROJ_COLS = 1024


def _sample_proj_kernel(x_ref, g_ref, w_ref, o_ref, wb_ref):
    h = _rms(x_ref[...], g_ref[...]).astype(BF16)
    wb = w_ref[...].astype(BF16)
    for q in range(SAMPLE_PROJ_COLS // HEAD_DIM):
        wb_ref[q] = wb[:, q * HEAD_DIM:(q + 1) * HEAD_DIM]
    o_ref[...] = _dot(h, wb)


def _sample_proj(x, g_mix, w_in):
    m = x.shape[0]
    slabs = SAMPLE_PROJ_COLS // HEAD_DIM
    return pl.pallas_call(
        _sample_proj_kernel,
        grid=(D_IN // SAMPLE_PROJ_COLS,),
        in_specs=[_const_spec((m, D_MODEL)), _const_spec((1, D_MODEL)),
                  pl.BlockSpec((D_MODEL, SAMPLE_PROJ_COLS), lambda j: (0, j))],
        out_specs=[pl.BlockSpec((m, SAMPLE_PROJ_COLS), lambda j: (0, j)),
                   pl.BlockSpec((slabs, D_MODEL, HEAD_DIM), lambda j: (j, 0, 0))],
        out_shape=[jax.ShapeDtypeStruct((m, D_IN), F32),
                   jax.ShapeDtypeStruct((D_IN // HEAD_DIM, D_MODEL, HEAD_DIM), BF16)],
        compiler_params=_params(1),
        name="sample_proj",
    )(x, g_mix, w_in)


def _head_rows(ref, bi, hd):
    halves = [ref[bi, pl.ds(half * N_HEADS + hd, N_MEM, stride=CACHE_ROWS), :]
              for half in range(HEAD_DIM // LANES)]
    return jnp.concatenate(halves, axis=-1)


def _sample_mixer_kernel(chunk_decay, l,
                         p_ref, cos_ref, sin_ref, s_ref, cbuf_ref, k_ref, v_ref,
                         gn_ref, cw_ref, cb_ref, lng_ref, lnb_ref, wco_ref,
                         dmat_ref, qdec_ref, kdec_ref,
                         merged_ref, sout_ref, cout_ref,
                         ext_scr, y_scr):
    rows = B_SAMPLE * l
    pad = 128
    row_id = lax.broadcasted_iota(jnp.int32, (rows, HEAD_DIM), 0)
    col_id = lax.broadcasted_iota(jnp.int32, (HEAD_DIM, pad), 1)
    row_of = [(row_id >= bi * l) & (row_id < (bi + 1) * l) for bi in range(B_SAMPLE)]
    col_of = [(col_id >= bi * l) & (col_id < (bi + 1) * l) for bi in range(B_SAMPLE)]

    def proj(off, n=HEAD_DIM):
        return p_ref[:, off:off + n]

    u = proj(OFF_GLU_A, D_MODEL) * jax.nn.sigmoid(proj(OFF_GLU_G, D_MODEL))
    for bi in range(B_SAMPLE):
        ext_scr[0:HIST, :] = cbuf_ref[bi]
        ext_scr[HIST:HIST + l, :] = u[bi * l:(bi + 1) * l, :]
        cout_ref[bi] = ext_scr[l:l + HIST, :]
        for i in range(l):
            y_scr[bi * l + i:bi * l + i + 1, :] = (
                jnp.sum(ext_scr[i:i + CONV_WIDTH, :] * cw_ref[...], axis=0, keepdims=True)
                + cb_ref[...])
    act = _silu(_standardize(y_scr[...]) * lng_ref[...] + lnb_ref[...])

    cos, sin = cos_ref[...], sin_ref[...]
    zeros_pad = jnp.zeros((pad - rows, HEAD_DIM), F32)

    def first_stage(hd):
        qb = _rope(proj(OFF_Q + hd * HEAD_DIM), cos, sin).astype(BF16)
        kr = _rope(proj(OFF_K + hd * HEAD_DIM), cos, sin) * QK_SCALE
        v = proj(OFF_V + hd * HEAD_DIM)
        kb_pad = jnp.concatenate([kr, zeros_pad], axis=0).astype(BF16)
        vb_pad = jnp.concatenate([v, zeros_pad], axis=0).astype(BF16)
        scores = _dot_nt(qb, kb_pad) * dmat_ref[hd]
        kd_t = jnp.concatenate([kr * kdec_ref[hd], zeros_pad], axis=0).T
        qx = proj(OFF_XA + hd * HEAD_DIM).astype(BF16)
        o_cross = jnp.zeros((rows, HEAD_DIM), F32)
        s = jnp.zeros((rows, N_MEM), F32)
        for bi in range(B_SAMPLE):
            s_old = s_ref[bi, hd]
            o_cross = jnp.where(row_of[bi], _dot(qb, s_old.astype(BF16)), o_cross)
            kd_bi = jnp.where(col_of[bi], kd_t, 0.0).astype(BF16)
            sout_ref[bi, hd] = s_old * chunk_decay[hd] + _dot(kd_bi, vb_pad)
            s = jnp.where(row_of[bi], _dot_nt(qx, _head_rows(k_ref, bi, hd).astype(BF16)), s)
        return scores, vb_pad, o_cross, s * QK_SCALE

    def second_stage(hd, scores, vb_pad, o_cross, s):
        sl = slice(hd * HEAD_DIM, (hd + 1) * HEAD_DIM)
        o = _dot(scores.astype(BF16), vb_pad) + o_cross * qdec_ref[hd]
        ret = _standardize(o) * gn_ref[:, sl] * _silu(proj(OFF_G + hd * HEAD_DIM))
        e = jnp.exp(s - jnp.max(s, axis=-1, keepdims=True))
        pb = (e / jnp.sum(e, axis=-1, keepdims=True)).astype(BF16)
        xa = jnp.zeros((rows, HEAD_DIM), F32)
        for bi in range(B_SAMPLE):
            xa = jnp.where(row_of[bi], _dot(pb, _head_rows(v_ref, bi, hd).astype(BF16)), xa)
        g0 = jax.nn.sigmoid(proj(OFF_GATE + hd * HEAD_DIM))
        g2 = jax.nn.sigmoid(proj(OFF_GATE + 2 * D_MODEL + hd * HEAD_DIM))
        return g0 * ret + g2 * xa

    parts = []
    staged = first_stage(0)
    for hd in range(N_HEADS):
        following = first_stage(hd + 1) if hd + 1 < N_HEADS else None
        parts.append(second_stage(hd, *staged))
        staged = following
    act = act.astype(BF16)
    conv = jnp.concatenate([_dot(act, wco_ref[n]) for n in range(D_MODEL // HEAD_DIM)], axis=-1)
    g1 = jax.nn.sigmoid(proj(OFF_GATE + D_MODEL, D_MODEL))
    merged_ref[...] = jnp.concatenate(parts, axis=-1) + g1 * conv


def _sample_mixer(proj, state_ret, state_conv, cache_k, cache_v, w, l):
    nb = state_ret.shape[0]
    rows = B_SAMPLE * l
    pad = 128
    cos, sin = _rope_tables(PAST_LEN + jnp.arange(l))
    cos, sin = jnp.tile(cos, (B_SAMPLE, 1)), jnp.tile(sin, (B_SAMPLE, 1))
    intra, q_decay, k_decay, chunk_decay = _decay_tables(_log_gammas(), l)
    same_req = (jnp.arange(rows)[:, None] // l) == (jnp.arange(pad)[None, :] // l)
    dmat = jnp.where(same_req[None], jnp.pad(jnp.tile(intra, (1, B_SAMPLE, B_SAMPLE)),
                                              ((0, 0), (0, 0), (0, pad - rows))), 0.0)
    qdec = jnp.broadcast_to(jnp.tile(q_decay, (1, B_SAMPLE))[:, :, None], (N_HEADS, rows, HEAD_DIM))
    kdec = jnp.broadcast_to(jnp.tile(k_decay, (1, B_SAMPLE))[:, :, None], (N_HEADS, rows, HEAD_DIM))

    def body(cd_ref, *refs):
        cd = tuple(cd_ref[i] for i in range(N_HEADS))
        _sample_mixer_kernel(cd, l, *refs)

    row = _const_spec((1, D_MODEL))
    state = pl.BlockSpec((B_SAMPLE, N_HEADS, HEAD_DIM, HEAD_DIM), lambda i: (i, 0, 0, 0))
    cbuf = pl.BlockSpec((B_SAMPLE, HIST, D_MODEL), lambda i: (i, 0, 0))
    cache = pl.BlockSpec((B_SAMPLE, N_MEM * CACHE_ROWS, LANES), lambda i: (i, 0, 0))
    return pl.pallas_call(
        body,
        grid=(nb // B_SAMPLE,),
        in_specs=[pl.BlockSpec(memory_space=pltpu.SMEM),
                  pl.BlockSpec((rows, D_IN), lambda i: (i, 0)),
                  _const_spec((rows, HALF)), _const_spec((rows, HALF)),
                  state, cbuf, cache, cache,
                  row, _const_spec((CONV_WIDTH, D_MODEL)), row, row, row,
                  _const_spec((D_MODEL // HEAD_DIM, D_MODEL, HEAD_DIM)),
                  _const_spec((N_HEADS, rows, pad)),
                  _const_spec((N_HEADS, rows, HEAD_DIM)), _const_spec((N_HEADS, rows, HEAD_DIM))],
        out_specs=[pl.BlockSpec((rows, D_MODEL), lambda i: (i, 0)), state, cbuf],
        out_shape=[jax.ShapeDtypeStruct((nb * l, D_MODEL), F32),
                   jax.ShapeDtypeStruct(state_ret.shape, F32),
                   jax.ShapeDtypeStruct(state_conv.shape, F32)],
        scratch_shapes=[pltpu.VMEM((HIST + l + 6, D_MODEL), F32),
                        pltpu.VMEM((rows, D_MODEL), F32)],
        compiler_params=_params(1),
        name="sample_mixer",
    )(chunk_decay, proj, cos, sin, state_ret, state_conv, cache_k, cache_v,
      w["ret_gn_g"], w["conv_w"], w["conv_b"], w["conv_ln_g"], w["conv_ln_b"], w["w_conv_out"],
      dmat, qdec, kdec)


def kernel(x_prompt, x_sample, mem_prompt, state_ret, state_conv, cache_mem_k, cache_mem_v, g_mix, w_in, ret_gn_g, conv_w, conv_b, conv_ln_g, conv_ln_b, w_conv_out, w_out, g_ffn, w_up, w_down, g_mem, w_mem_kv, g_final):
    assert state_ret.shape[0] == 1, "one layer"
    b_p, l_p, _ = x_prompt.shape
    b_s, l_s, _ = x_sample.shape
    w = {
        "g_mix": g_mix[0][None], "ret_gn_g": ret_gn_g[0][None],
        "conv_w": conv_w[0], "conv_b": conv_b[0][None], "conv_ln_g": conv_ln_g[0][None],
        "conv_ln_b": conv_ln_b[0][None], "w_conv_out": _column_slabs(w_conv_out[0].astype(BF16)),
        "g_ffn": g_ffn[0][None], "g_final": g_final[None],
    }

    xs = x_sample.reshape(b_s * l_s, D_MODEL)
    proj_s, w["w_in"] = _sample_proj(xs, w["g_mix"], w_in[0])
    merged_s, ret_s, conv_s = _sample_mixer(
        proj_s, state_ret[0], state_conv[0],
        _cache_rows(cache_mem_k[0]), _cache_rows(cache_mem_v[0]),
        w, l_s)
    y_sample, w["w_out"], w["w_up"], w["w_down"] = _ffn_stream(
        xs, merged_s, w_out[0], w["g_ffn"], w_up[0], w_down[0], w["g_final"])

    mk, mv, mkb, mvb = _memkv(mem_prompt, g_mem[0][None], w_mem_kv[0])
    merged_p, ret_p, conv_p = _prompt_mixer(x_prompt, mkb, mvb, w)
    y_prompt = _ffn(x_prompt.reshape(b_p * l_p, D_MODEL), merged_p.reshape(b_p * l_p, D_MODEL), w)

    return (y_prompt.reshape(b_p, l_p, D_MODEL), y_sample.reshape(b_s, l_s, D_MODEL),
            ret_p[None], conv_p[None], mk[None], mv[None], ret_s[None], conv_s[None])
```

```python
import functools

import jax
import jax.numpy as jnp
from jax import lax
from jax.experimental import pallas as pl
from jax.experimental.pallas import tpu as pltpu

F32 = jnp.float32
BF16 = jnp.bfloat16

D_MODEL = 1024
N_HEADS = 4
HEAD_DIM = 256
HALF = HEAD_DIM // 2
D_FF = 4 * D_MODEL
CONV_WIDTH = 31
HIST = CONV_WIDTH - 1
N_MEM = 256
PAST_LEN = 16384
ROPE_BASE = 10000.0
EPS = 1e-6
QK_SCALE = HEAD_DIM ** -0.5

OFF_Q, OFF_K, OFF_V, OFF_G = 0, 1024, 2048, 3072
OFF_GLU_A, OFF_GLU_G, OFF_XA, OFF_GATE = 4096, 5120, 6144, 7168
D_IN = 10240

SUBLANES = 8
HIST_PAD = 32
T_PROMPT = 256
T_FFN = 512
FFN_ROWS = 256
B_SAMPLE = 4
VMEM_LIMIT = 56 * 1024 * 1024

LANES = 128
CACHE_ROWS = N_HEADS * HEAD_DIM // LANES

NT_DIMS = (((1,), (1,)), ((), ()))


def _cache_rows(c):
    b = c.shape[0]
    c = c.reshape(b, N_MEM, N_HEADS, HEAD_DIM // LANES, LANES).transpose(0, 1, 3, 2, 4)
    return c.reshape(b, N_MEM * CACHE_ROWS, LANES)


def _column_slabs(w):
    k, n = w.shape
    return w.reshape(k, n // HEAD_DIM, HEAD_DIM).transpose(1, 0, 2)


def _cache_from_rows(r):
    b = r.shape[0]
    r = r.reshape(b, N_MEM, HEAD_DIM // LANES, N_HEADS, LANES).transpose(0, 1, 3, 2, 4)
    return r.reshape(b, N_MEM, N_HEADS, HEAD_DIM)


def _dot(a, b):
    return jnp.dot(a, b, preferred_element_type=F32)


def _dot_nt(a, b):
    return lax.dot_general(a, b, NT_DIMS, preferred_element_type=F32)


def _rms(x, g):
    return x * lax.rsqrt(jnp.mean(x * x, axis=-1, keepdims=True) + EPS) * g


def _standardize(x):
    mu = jnp.mean(x, axis=-1, keepdims=True)
    xc = x - mu
    return xc * lax.rsqrt(jnp.mean(xc * xc, axis=-1, keepdims=True) + EPS)


def _silu(x):
    return x * jax.nn.sigmoid(x)


def _rope(x, cos, sin):
    x1, x2 = x[:, :HALF], x[:, HALF:]
    return jnp.concatenate([x1 * cos - x2 * sin, x2 * cos + x1 * sin], axis=-1)


def _const_spec(shape):
    return pl.BlockSpec(shape, lambda *_: (0,) * len(shape), pipeline_mode=pl.Buffered(1))


def _params(n_grid):
    return pltpu.CompilerParams(dimension_semantics=("arbitrary",) * n_grid,
                                vmem_limit_bytes=VMEM_LIMIT)


def _memkv_kernel(mem_ref, g_ref, w_ref, k_ref, v_ref, kb_ref, vb_ref, wb_scr):
    @pl.when(pl.program_id(0) == 0)
    def _():
        wb_scr[...] = w_ref[...].astype(BF16)

    h = _rms(mem_ref[0], g_ref[...]).astype(BF16)
    kv = _dot(h, wb_scr[...])
    k, v = kv[:, :D_MODEL], kv[:, D_MODEL:]
    for hd in range(N_HEADS):
        kb_ref[0, hd] = k[:, hd * HEAD_DIM:(hd + 1) * HEAD_DIM].astype(BF16)
        vb_ref[0, hd] = v[:, hd * HEAD_DIM:(hd + 1) * HEAD_DIM].astype(BF16)
    for hd in range(N_HEADS):
        for half in range(HEAD_DIM // LANES):
            rows = pl.ds(half * N_HEADS + hd, N_MEM, stride=CACHE_ROWS)
            cols = slice(hd * HEAD_DIM + half * LANES, hd * HEAD_DIM + (half + 1) * LANES)
            k_ref[0, rows, :] = k[:, cols]
            v_ref[0, rows, :] = v[:, cols]


def _memkv(mem, g_mem, w_mem_kv):
    b = mem.shape[0]
    blk = pl.BlockSpec((1, N_MEM, D_MODEL), lambda i: (i, 0, 0))
    rows = pl.BlockSpec((1, N_MEM * CACHE_ROWS, LANES), lambda i: (i, 0, 0))
    slabs = pl.BlockSpec((1, N_HEADS, N_MEM, HEAD_DIM), lambda i: (i, 0, 0, 0))
    k, v, kb, vb = pl.pallas_call(
        _memkv_kernel,
        grid=(b,),
        in_specs=[blk, _const_spec((1, D_MODEL)), _const_spec((D_MODEL, 2 * D_MODEL))],
        out_specs=[rows, rows, slabs, slabs],
        out_shape=[jax.ShapeDtypeStruct((b, N_MEM * CACHE_ROWS, LANES), F32)] * 2
        + [jax.ShapeDtypeStruct((b, N_HEADS, N_MEM, HEAD_DIM), BF16)] * 2,
        scratch_shapes=[pltpu.VMEM((D_MODEL, 2 * D_MODEL), BF16)],
        compiler_params=_params(1),
        name="memkv",
    )(mem, g_mem, w_mem_kv)
    return _cache_from_rows(k), _cache_from_rows(v), kb, vb


CONV_ROWS = 128


def _depthwise_conv(ext_ref, cw_ref, cb_ref, y_ref, t, lane_slices):
    first = HIST_PAD - HIST
    for j in lane_slices:
        cols = slice(j * LANES, (j + 1) * LANES)
        for r0 in range(0, t, CONV_ROWS):
            acc = jnp.broadcast_to(cb_ref[:, cols], (CONV_ROWS, LANES))
            for r in range(SUBLANES):
                rows = CONV_ROWS + (SUBLANES if r else 0)
                part = None
                for off in range(r, first + CONV_WIDTH, SUBLANES):
                    if off < first:
                        continue
                    w = off - first
                    base = r0 + off - r
                    term = ext_ref[base:base + rows, cols] * cw_ref[w:w + 1, cols]
                    part = term if part is None else part + term
                acc = acc + part[r:r + CONV_ROWS]
            y_ref[r0:r0 + CONV_ROWS, cols] = acc


def _prompt_mixer_kernel(chunk_decay,
                         x_ref, cos_ref, sin_ref, gmix_ref, win_ref, gn_ref, cw_ref, cb_ref,
                         lng_ref, lnb_ref, wco_ref, kb_ref, vb_ref, dmat_ref, qdec_ref, kdec_ref,
                         merged_ref, s_ref, cout_ref,
                         ext_scr, y_scr, conv_scr, part_scr, g1_scr):
    t = T_PROMPT

    @pl.when(pl.program_id(1) == 0)
    def _():
        s_ref[...] = jnp.zeros_like(s_ref)
        ext_scr[0:HIST_PAD, :] = jnp.zeros((HIST_PAD, D_MODEL), F32)

    h = _rms(x_ref[0], gmix_ref[...]).astype(BF16)

    def proj(off):
        return _dot(h, win_ref[off // HEAD_DIM])

    for j in range(D_MODEL // HEAD_DIM):
        cols = slice(j * HEAD_DIM, (j + 1) * HEAD_DIM)
        ext_scr[HIST_PAD:HIST_PAD + t, cols] = (
            proj(OFF_GLU_A + j * HEAD_DIM) * jax.nn.sigmoid(proj(OFF_GLU_G + j * HEAD_DIM)))

    cos, sin = cos_ref[...], sin_ref[...]
    conv_slices = D_MODEL // LANES // N_HEADS
    last = N_HEADS - 1
    for hd in range(N_HEADS):
        sl = slice(hd * HEAD_DIM, (hd + 1) * HEAD_DIM)
        _depthwise_conv(ext_scr, cw_ref, cb_ref, y_scr, t,
                        range(hd * conv_slices, (hd + 1) * conv_slices))
        qb = _rope(proj(OFF_Q + hd * HEAD_DIM), cos, sin).astype(BF16)
        kr = _rope(proj(OFF_K + hd * HEAD_DIM), cos, sin) * QK_SCALE
        vb = proj(OFF_V + hd * HEAD_DIM).astype(BF16)
        swish_g = _silu(proj(OFF_G + hd * HEAD_DIM))
        scores = _dot_nt(qb, kr.astype(BF16)) * dmat_ref[hd]
        qx = proj(OFF_XA + hd * HEAD_DIM).astype(BF16)
        s_old = s_ref[0, hd]
        o_cross = _dot(qb, s_old.astype(BF16)) * qdec_ref[hd]
        g0 = jax.nn.sigmoid(proj(OFF_GATE + hd * HEAD_DIM))
        o = _dot(scores.astype(BF16), vb) + o_cross
        s = _dot_nt(qx, kb_ref[0, hd]) * QK_SCALE
        g2 = jax.nn.sigmoid(proj(OFF_GATE + 2 * D_MODEL + hd * HEAD_DIM))
        kd_t = (kr * kdec_ref[hd]).T.astype(BF16)
        s_ref[0, hd] = s_old * chunk_decay[hd] + _dot(kd_t, vb)
        g1 = jax.nn.sigmoid(proj(OFF_GATE + D_MODEL + hd * HEAD_DIM))
        if hd == last:
            act = _silu(_standardize(y_scr[...]) * lng_ref[...] + lnb_ref[...])
            act = act.astype(BF16)
            for n in range(D_MODEL // HEAD_DIM):
                conv_scr[:, n * HEAD_DIM:(n + 1) * HEAD_DIM] = _dot(act, wco_ref[n])
            done = slice(0, hd * HEAD_DIM)
            merged_ref[0, :, done] = (
                part_scr[:, done] + g1_scr[:, done] * conv_scr[:, done]).astype(BF16)
        e = jnp.exp(s - jnp.max(s, axis=-1, keepdims=True))
        p = e / jnp.sum(e, axis=-1, keepdims=True)
        xa = _dot(p.astype(BF16), vb_ref[0, hd])
        ret = _standardize(o) * gn_ref[:, sl] * swish_g
        part = g0 * ret + g2 * xa
        if hd < last:
            part_scr[:, sl] = part
            g1_scr[:, sl] = g1
        else:
            merged_ref[0, :, sl] = (part + g1 * conv_scr[:, sl]).astype(BF16)

    cout_ref[0] = ext_scr[t + HIST_PAD - HIST:t + HIST_PAD, :]
    ext_scr[0:HIST_PAD, :] = ext_scr[t:t + HIST_PAD, :]


def _decay_tables(log_gamma, c):
    idx = jnp.arange(c, dtype=F32)
    diff = idx[:, None] - idx[None, :]
    causal = diff >= 0
    intra = jnp.where(causal[None],
                      jnp.exp(log_gamma[:, None, None] * jnp.where(causal, diff, 0.0)[None]), 0.0)
    q_decay = jnp.exp(log_gamma[:, None] * (idx + 1.0))
    k_decay = jnp.exp(log_gamma[:, None] * (c - 1.0 - idx))
    chunk_decay = jnp.exp(log_gamma * c)
    return intra, q_decay, k_decay, chunk_decay


def _log_gammas():
    return jnp.log1p(-jnp.exp2(-5.0 - jnp.arange(N_HEADS, dtype=F32)))


def _rope_tables(pos):
    inv = ROPE_BASE ** (-jnp.arange(0, HEAD_DIM, 2, dtype=F32) / HEAD_DIM)
    ang = pos.astype(F32)[:, None] * inv[None, :]
    return jnp.cos(ang), jnp.sin(ang)


def _prompt_mixer(x, kb, vb, w):
    b, l, _ = x.shape
    t = T_PROMPT
    n_chunks = l // t
    cos, sin = _rope_tables(jnp.arange(l))
    intra, q_decay, k_decay, chunk_decay = _decay_tables(_log_gammas(), t)
    qdec = jnp.broadcast_to(q_decay[:, :, None], (N_HEADS, t, HEAD_DIM))
    kdec = jnp.broadcast_to(k_decay[:, :, None], (N_HEADS, t, HEAD_DIM))

    def body(cd_ref, *refs):
        cd = tuple(cd_ref[i] for i in range(N_HEADS))
        _prompt_mixer_kernel(cd, *refs)

    tok = pl.BlockSpec((1, t, D_MODEL), lambda i, j: (i, j, 0))
    rope = pl.BlockSpec((t, HALF), lambda i, j: (j, 0))
    mem = pl.BlockSpec((1, N_HEADS, N_MEM, HEAD_DIM), lambda i, j: (i, 0, 0, 0))
    row = _const_spec((1, D_MODEL))
    table = _const_spec((N_HEADS, t, HEAD_DIM))
    return pl.pallas_call(
        body,
        grid=(b, n_chunks),
        in_specs=[pl.BlockSpec(memory_space=pltpu.SMEM),
                  tok, rope, rope, row, _const_spec((D_IN // HEAD_DIM, D_MODEL, HEAD_DIM)), row,
                  _const_spec((CONV_WIDTH, D_MODEL)), row, row, row,
                  _const_spec((D_MODEL // HEAD_DIM, D_MODEL, HEAD_DIM)), mem, mem,
                  table, table, table],
        out_specs=[tok,
                   pl.BlockSpec((1, N_HEADS, HEAD_DIM, HEAD_DIM), lambda i, j: (i, 0, 0, 0)),
                   pl.BlockSpec((1, HIST, D_MODEL), lambda i, j: (i, 0, 0))],
        out_shape=[jax.ShapeDtypeStruct((b, l, D_MODEL), BF16),
                   jax.ShapeDtypeStruct((b, N_HEADS, HEAD_DIM, HEAD_DIM), F32),
                   jax.ShapeDtypeStruct((b, HIST, D_MODEL), F32)],
        scratch_shapes=[pltpu.VMEM((HIST_PAD + t, D_MODEL), F32)]
        + [pltpu.VMEM((t, D_MODEL), F32)] * 4,
        compiler_params=_params(2),
        name="prompt_mixer",
    )(chunk_decay, x, cos, sin, w["g_mix"], w["w_in"], w["ret_gn_g"], w["conv_w"],
      w["conv_b"], w["conv_ln_g"], w["conv_ln_b"], w["w_conv_out"], kb, vb, intra, qdec, kdec)


def _ffn_kernel(x_ref, m_ref, wout_ref, gffn_ref, wup_ref, wdown_ref, gfin_ref, y_ref):
    groups = [slice(r, r + FFN_ROWS) for r in range(0, T_FFN, FFN_ROWS)]
    x1 = [x_ref[g, :] + _dot(m_ref[g, :].astype(BF16), wout_ref[...]) for g in groups]
    h2 = [_rms(v, gffn_ref[...]).astype(BF16) for v in x1]
    a = [jnp.square(jnp.maximum(_dot(v, wup_ref[...]), 0.0)).astype(BF16) for v in h2]
    for g, v, act in zip(groups, x1, a):
        y_ref[g, :] = _rms(v + _dot(act, wdown_ref[...]), gfin_ref[...])


def _ffn(x, merged, w):
    m = x.shape[0]
    tok = pl.BlockSpec((T_FFN, D_MODEL), lambda i: (i, 0))
    row = _const_spec((1, D_MODEL))
    return pl.pallas_call(
        _ffn_kernel,
        grid=(m // T_FFN,),
        in_specs=[tok, tok, _const_spec((D_MODEL, D_MODEL)), row, _const_spec((D_MODEL, D_FF)),
                  _const_spec((D_FF, D_MODEL)), row],
        out_specs=tok,
        out_shape=jax.ShapeDtypeStruct((m, D_MODEL), F32),
        compiler_params=_params(1),
        name="ffn",
    )(x, merged, w["w_out"], w["g_ffn"], w["w_up"], w["w_down"], w["g_final"])


FF_BLOCK = 512


def _ffn_stream_kernel(x_ref, m_ref, wout_ref, gffn_ref, wup_ref, wdown_ref, gfin_ref,
                       y_ref, wout_b_ref, wup_b_ref, wdown_b_ref,
                       x1_scr, h2_scr, acc_scr):
    j = pl.program_id(0)

    @pl.when(j == 0)
    def _():
        wout = wout_ref[...].astype(BF16)
        wout_b_ref[...] = wout
        x1 = x_ref[...] + _dot(m_ref[...].astype(BF16), wout)
        x1_scr[...] = x1
        h2_scr[...] = _rms(x1, gffn_ref[...]).astype(BF16)
        acc_scr[...] = jnp.zeros_like(acc_scr)

    wup = wup_ref[...].astype(BF16)
    wdown = wdown_ref[...].astype(BF16)
    wup_b_ref[...] = wup
    wdown_b_ref[...] = wdown
    a = jnp.square(jnp.maximum(_dot(h2_scr[...], wup), 0.0)).astype(BF16)
    acc_scr[...] += _dot(a, wdown)

    @pl.when(j == pl.num_programs(0) - 1)
    def _():
        y_ref[...] = _rms(x1_scr[...] + acc_scr[...], gfin_ref[...])


def _ffn_stream(x, merged, w_out, g_ffn, w_up, w_down, g_final):
    m = x.shape[0]
    full = _const_spec((m, D_MODEL))
    row = _const_spec((1, D_MODEL))
    sq = _const_spec((D_MODEL, D_MODEL))
    up = pl.BlockSpec((D_MODEL, FF_BLOCK), lambda j: (0, j))
    down = pl.BlockSpec((FF_BLOCK, D_MODEL), lambda j: (j, 0))
    return pl.pallas_call(
        _ffn_stream_kernel,
        grid=(D_FF // FF_BLOCK,),
        in_specs=[full, full, sq, row, up, down, row],
        out_specs=[pl.BlockSpec((m, D_MODEL), lambda j: (0, 0)),
                   pl.BlockSpec((D_MODEL, D_MODEL), lambda j: (0, 0)), up, down],
        out_shape=[jax.ShapeDtypeStruct((m, D_MODEL), F32),
                   jax.ShapeDtypeStruct((D_MODEL, D_MODEL), BF16),
                   jax.ShapeDtypeStruct((D_MODEL, D_FF), BF16),
                   jax.ShapeDtypeStruct((D_FF, D_MODEL), BF16)],
        scratch_shapes=[pltpu.VMEM((m, D_MODEL), F32), pltpu.VMEM((m, D_MODEL), BF16),
                        pltpu.VMEM((m, D_MODEL), F32)],
        compiler_params=_params(1),
        name="ffn_stream",
    )(x, merged, w_out, g_ffn, w_up, w_down, g_final)


SAMPLE_PROJ_COLS = 1024


def _sample_proj_kernel(x_ref, g_ref, w_ref, o_ref, wb_ref):
    h = _rms(x_ref[...], g_ref[...]).astype(BF16)
    wb = w_ref[...].astype(BF16)
    for q in range(SAMPLE_PROJ_COLS // HEAD_DIM):
        wb_ref[q] = wb[:, q * HEAD_DIM:(q + 1) * HEAD_DIM]
    o_ref[...] = _dot(h, wb)


def _sample_proj(x, g_mix, w_in):
    m = x.shape[0]
    slabs = SAMPLE_PROJ_COLS // HEAD_DIM
    return pl.pallas_call(
        _sample_proj_kernel,
        grid=(D_IN // SAMPLE_PROJ_COLS,),
        in_specs=[_const_spec((m, D_MODEL)), _const_spec((1, D_MODEL)),
                  pl.BlockSpec((D_MODEL, SAMPLE_PROJ_COLS), lambda j: (0, j))],
        out_specs=[pl.BlockSpec((m, SAMPLE_PROJ_COLS), lambda j: (0, j)),
                   pl.BlockSpec((slabs, D_MODEL, HEAD_DIM), lambda j: (j, 0, 0))],
        out_shape=[jax.ShapeDtypeStruct((m, D_IN), F32),
                   jax.ShapeDtypeStruct((D_IN // HEAD_DIM, D_MODEL, HEAD_DIM), BF16)],
        compiler_params=_params(1),
        name="sample_proj",
    )(x, g_mix, w_in)


def _head_rows(ref, bi, hd):
    halves = [ref[bi, pl.ds(half * N_HEADS + hd, N_MEM, stride=CACHE_ROWS), :]
              for half in range(HEAD_DIM // LANES)]
    return jnp.concatenate(halves, axis=-1)


SAMPLE_CONV_LANES = LANES


def _sample_conv_kernel(l, a_ref, g_ref, st_ref, cw_ref, cb_ref, lng_ref, lnb_ref, wco_ref,
                        conv_ref, st_out_ref, u_scr, y_scr):
    j = pl.program_id(0)
    nb = st_ref.shape[1]
    u_scr[...] = a_ref[...] * jax.nn.sigmoid(g_ref[...])

    def plane(k):
        return st_ref[k] if k < HIST else u_scr[pl.ds(k - HIST, nb, stride=l), :]

    for w in range(HIST):
        st_out_ref[w] = plane(w + l)
    for i in range(l):
        acc = jnp.broadcast_to(cb_ref[...], (nb, SAMPLE_CONV_LANES))
        for w in range(CONV_WIDTH):
            acc = acc + plane(i + w) * cw_ref[w:w + 1, :]
        y_scr[j, pl.ds(i, nb, stride=l), :] = acc

    @pl.when(j == pl.num_programs(0) - 1)
    def _():
        y = jnp.concatenate([y_scr[n] for n in range(D_MODEL // SAMPLE_CONV_LANES)], axis=-1)
        act = _silu(_standardize(y) * lng_ref[...] + lnb_ref[...]).astype(BF16)
        for n in range(D_MODEL // HEAD_DIM):
            conv_ref[:, n * HEAD_DIM:(n + 1) * HEAD_DIM] = _dot(act, wco_ref[n])


def _sample_conv(proj, state_t, w, l):
    m = proj.shape[0]
    nb = state_t.shape[1]
    lanes = SAMPLE_CONV_LANES
    glu_a = pl.BlockSpec((m, lanes), lambda j: (0, OFF_GLU_A // lanes + j))
    glu_g = pl.BlockSpec((m, lanes), lambda j: (0, OFF_GLU_G // lanes + j))
    st = pl.BlockSpec((HIST, nb, lanes), lambda j: (0, 0, j))
    row_blk = pl.BlockSpec((1, lanes), lambda j: (0, j))
    return pl.pallas_call(
        functools.partial(_sample_conv_kernel, l),
        grid=(D_MODEL // lanes,),
        in_specs=[glu_a, glu_g, st, pl.BlockSpec((CONV_WIDTH, lanes), lambda j: (0, j)), row_blk,
                  _const_spec((1, D_MODEL)), _const_spec((1, D_MODEL)),
                  _const_spec((D_MODEL // HEAD_DIM, D_MODEL, HEAD_DIM))],
        out_specs=[pl.BlockSpec((m, D_MODEL), lambda j: (0, 0)), st],
        out_shape=[jax.ShapeDtypeStruct((m, D_MODEL), F32),
                   jax.ShapeDtypeStruct(state_t.shape, F32)],
        scratch_shapes=[pltpu.VMEM((m, lanes), F32),
                        pltpu.VMEM((D_MODEL // lanes, m, lanes), F32)],
        compiler_params=_params(1),
        name="sample_conv",
    )(proj, proj, state_t, w["conv_w"], w["conv_b"], w["conv_ln_g"], w["conv_ln_b"],
      w["w_conv_out"])


def _sample_mixer_kernel(chunk_decay, l,
                         p_ref, conv_ref, cos_ref, sin_ref, s_ref, k_ref, v_ref,
                         gn_ref, dmat_ref, qdec_ref, kdec_ref,
                         merged_ref, sout_ref):
    rows = B_SAMPLE * l
    pad = 128
    row_id = lax.broadcasted_iota(jnp.int32, (rows, HEAD_DIM), 0)
    col_id = lax.broadcasted_iota(jnp.int32, (HEAD_DIM, pad), 1)
    row_of = [(row_id >= bi * l) & (row_id < (bi + 1) * l) for bi in range(B_SAMPLE)]
    col_of = [(col_id >= bi * l) & (col_id < (bi + 1) * l) for bi in range(B_SAMPLE)]

    def proj(off, n=HEAD_DIM):
        return p_ref[:, off:off + n]

    cos, sin = cos_ref[...], sin_ref[...]
    zeros_pad = jnp.zeros((pad - rows, HEAD_DIM), F32)

    def first_stage(hd):
        qb = _rope(proj(OFF_Q + hd * HEAD_DIM), cos, sin).astype(BF16)
        kr = _rope(proj(OFF_K + hd * HEAD_DIM), cos, sin) * QK_SCALE
        v = proj(OFF_V + hd * HEAD_DIM)
        kb_pad = jnp.concatenate([kr, zeros_pad], axis=0).astype(BF16)
        vb_pad = jnp.concatenate([v, zeros_pad], axis=0).astype(BF16)
        scores = _dot_nt(qb, kb_pad) * dmat_ref[hd]
        kd_t = jnp.concatenate([kr * kdec_ref[hd], zeros_pad], axis=0).T
        qx = proj(OFF_XA + hd * HEAD_DIM).astype(BF16)
        o_cross = jnp.zeros((rows, HEAD_DIM), F32)
        s = jnp.zeros((rows, N_MEM), F32)
        for bi in range(B_SAMPLE):
            s_old = s_ref[bi, hd]
            o_cross = jnp.where(row_of[bi], _dot(qb, s_old.astype(BF16)), o_cross)
            kd_bi = jnp.where(col_of[bi], kd_t, 0.0).astype(BF16)
            sout_ref[bi, hd] = s_old * chunk_decay[hd] + _dot(kd_bi, vb_pad)
            s = jnp.where(row_of[bi], _dot_nt(qx, _head_rows(k_ref, bi, hd).astype(BF16)), s)
        return scores, vb_pad, o_cross, s * QK_SCALE

    def second_stage(hd, scores, vb_pad, o_cross, s):
        sl = slice(hd * HEAD_DIM, (hd + 1) * HEAD_DIM)
        o = _dot(scores.astype(BF16), vb_pad) + o_cross * qdec_ref[hd]
        ret = _standardize(o) * gn_ref[:, sl] * _silu(proj(OFF_G + hd * HEAD_DIM))
        e = jnp.exp(s - jnp.max(s, axis=-1, keepdims=True))
        pb = (e / jnp.sum(e, axis=-1, keepdims=True)).astype(BF16)
        xa = jnp.zeros((rows, HEAD_DIM), F32)
        for bi in range(B_SAMPLE):
            xa = jnp.where(row_of[bi], _dot(pb, _head_rows(v_ref, bi, hd).astype(BF16)), xa)
        g0 = jax.nn.sigmoid(proj(OFF_GATE + hd * HEAD_DIM))
        g2 = jax.nn.sigmoid(proj(OFF_GATE + 2 * D_MODEL + hd * HEAD_DIM))
        return g0 * ret + g2 * xa

    parts = []
    staged = first_stage(0)
    for hd in range(N_HEADS):
        following = first_stage(hd + 1) if hd + 1 < N_HEADS else None
        parts.append(second_stage(hd, *staged))
        staged = following
    g1 = jax.nn.sigmoid(proj(OFF_GATE + D_MODEL, D_MODEL))
    merged_ref[...] = jnp.concatenate(parts, axis=-1) + g1 * conv_ref[...]


def _sample_mixer(proj, conv, state_ret, cache_k, cache_v, w, l):
    nb = state_ret.shape[0]
    rows = B_SAMPLE * l
    pad = 128
    cos, sin = _rope_tables(PAST_LEN + jnp.arange(l))
    cos, sin = jnp.tile(cos, (B_SAMPLE, 1)), jnp.tile(sin, (B_SAMPLE, 1))
    intra, q_decay, k_decay, chunk_decay = _decay_tables(_log_gammas(), l)
    same_req = (jnp.arange(rows)[:, None] // l) == (jnp.arange(pad)[None, :] // l)
    dmat = jnp.where(same_req[None], jnp.pad(jnp.tile(intra, (1, B_SAMPLE, B_SAMPLE)),
                                              ((0, 0), (0, 0), (0, pad - rows))), 0.0)
    qdec = jnp.broadcast_to(jnp.tile(q_decay, (1, B_SAMPLE))[:, :, None], (N_HEADS, rows, HEAD_DIM))
    kdec = jnp.broadcast_to(jnp.tile(k_decay, (1, B_SAMPLE))[:, :, None], (N_HEADS, rows, HEAD_DIM))

    def body(cd_ref, *refs):
        cd = tuple(cd_ref[i] for i in range(N_HEADS))
        _sample_mixer_kernel(cd, l, *refs)

    row = _const_spec((1, D_MODEL))
    state = pl.BlockSpec((B_SAMPLE, N_HEADS, HEAD_DIM, HEAD_DIM), lambda i: (i, 0, 0, 0))
    tok = pl.BlockSpec((rows, D_MODEL), lambda i: (i, 0))
    cache = pl.BlockSpec((B_SAMPLE, N_MEM * CACHE_ROWS, LANES), lambda i: (i, 0, 0))
    return pl.pallas_call(
        body,
        grid=(nb // B_SAMPLE,),
        in_specs=[pl.BlockSpec(memory_space=pltpu.SMEM),
                  pl.BlockSpec((rows, D_IN), lambda i: (i, 0)), tok,
                  _const_spec((rows, HALF)), _const_spec((rows, HALF)),
                  state, cache, cache, row,
                  _const_spec((N_HEADS, rows, pad)),
                  _const_spec((N_HEADS, rows, HEAD_DIM)), _const_spec((N_HEADS, rows, HEAD_DIM))],
        out_specs=[tok, state],
        out_shape=[jax.ShapeDtypeStruct((nb * l, D_MODEL), F32),
                   jax.ShapeDtypeStruct(state_ret.shape, F32)],
        compiler_params=_params(1),
        name="sample_mixer",
    )(chunk_decay, proj, conv, cos, sin, state_ret, cache_k, cache_v, w["ret_gn_g"],
      dmat, qdec, kdec)


def kernel(x_prompt, x_sample, mem_prompt, state_ret, state_conv, cache_mem_k, cache_mem_v, g_mix, w_in, ret_gn_g, conv_w, conv_b, conv_ln_g, conv_ln_b, w_conv_out, w_out, g_ffn, w_up, w_down, g_mem, w_mem_kv, g_final):
    assert state_ret.shape[0] == 1, "one layer"
    b_p, l_p, _ = x_prompt.shape
    b_s, l_s, _ = x_sample.shape
    w = {
        "g_mix": g_mix[0][None], "ret_gn_g": ret_gn_g[0][None],
        "conv_w": conv_w[0], "conv_b": conv_b[0][None], "conv_ln_g": conv_ln_g[0][None],
        "conv_ln_b": conv_ln_b[0][None], "w_conv_out": _column_slabs(w_conv_out[0].astype(BF16)),
        "g_ffn": g_ffn[0][None], "g_final": g_final[None],
    }

    xs = x_sample.reshape(b_s * l_s, D_MODEL)
    proj_s, w["w_in"] = _sample_proj(xs, w["g_mix"], w_in[0])
    conv_out_s, conv_s = _sample_conv(proj_s, jnp.transpose(state_conv[0], (1, 0, 2)), w, l_s)
    conv_s = jnp.transpose(conv_s, (1, 0, 2))
    merged_s, ret_s = _sample_mixer(
        proj_s, conv_out_s, state_ret[0],
        _cache_rows(cache_mem_k[0]), _cache_rows(cache_mem_v[0]),
        w, l_s)
    y_sample, w["w_out"], w["w_up"], w["w_down"] = _ffn_stream(
        xs, merged_s, w_out[0], w["g_ffn"], w_up[0], w_down[0], w["g_final"])

    mk, mv, mkb, mvb = _memkv(mem_prompt, g_mem[0][None], w_mem_kv[0])
    merged_p, ret_p, conv_p = _prompt_mixer(x_prompt, mkb, mvb, w)
    y_prompt = _ffn(x_prompt.reshape(b_p * l_p, D_MODEL), merged_p.reshape(b_p * l_p, D_MODEL), w)

    return (y_prompt.reshape(b_p, l_p, D_MODEL), y_sample.reshape(b_s, l_s, D_MODEL),
            ret_p[None], conv_p[None], mk[None], mv[None], ret_s[None], conv_s[None])
```

```python
import functools

import jax
import jax.numpy as jnp
from jax import lax
from jax.experimental import pallas as pl
from jax.experimental.pallas import tpu as pltpu

F32 = jnp.float32
BF16 = jnp.bfloat16

D_MODEL = 1024
N_HEADS = 4
HEAD_DIM = 256
HALF = HEAD_DIM // 2
D_FF = 4 * D_MODEL
CONV_WIDTH = 31
HIST = CONV_WIDTH - 1
N_MEM = 256
PAST_LEN = 16384
ROPE_BASE = 10000.0
EPS = 1e-6
QK_SCALE = HEAD_DIM ** -0.5

OFF_Q, OFF_K, OFF_V, OFF_G = 0, 1024, 2048, 3072
OFF_GLU_A, OFF_GLU_G, OFF_XA, OFF_GATE = 4096, 5120, 6144, 7168
D_IN = 10240

SUBLANES = 8
HIST_PAD = 32
T_PROMPT = 256
T_FFN = 512
FFN_ROWS = 256
B_SAMPLE = 2
VMEM_LIMIT = 56 * 1024 * 1024

LANES = 128
CACHE_ROWS = N_HEADS * HEAD_DIM // LANES

NT_DIMS = (((1,), (1,)), ((), ()))


def _cache_rows(c):
    b = c.shape[0]
    c = c.reshape(b, N_MEM, N_HEADS, HEAD_DIM // LANES, LANES).transpose(0, 1, 3, 2, 4)
    return c.reshape(b, N_MEM * CACHE_ROWS, LANES)


def _column_slabs(w):
    k, n = w.shape
    return w.reshape(k, n // HEAD_DIM, HEAD_DIM).transpose(1, 0, 2)


def _cache_from_rows(r):
    b = r.shape[0]
    r = r.reshape(b, N_MEM, HEAD_DIM // LANES, N_HEADS, LANES).transpose(0, 1, 3, 2, 4)
    return r.reshape(b, N_MEM, N_HEADS, HEAD_DIM)


def _dot(a, b):
    return jnp.dot(a, b, preferred_element_type=F32)


def _dot_nt(a, b):
    return lax.dot_general(a, b, NT_DIMS, preferred_element_type=F32)


def _rms(x, g):
    return x * lax.rsqrt(jnp.mean(x * x, axis=-1, keepdims=True) + EPS) * g


def _standardize(x):
    mu = jnp.mean(x, axis=-1, keepdims=True)
    xc = x - mu
    return xc * lax.rsqrt(jnp.mean(xc * xc, axis=-1, keepdims=True) + EPS)


def _silu(x):
    return x * jax.nn.sigmoid(x)


def _rope(x, cos, sin):
    x1, x2 = x[:, :HALF], x[:, HALF:]
    return jnp.concatenate([x1 * cos - x2 * sin, x2 * cos + x1 * sin], axis=-1)


def _const_spec(shape):
    return pl.BlockSpec(shape, lambda *_: (0,) * len(shape), pipeline_mode=pl.Buffered(1))


def _params(n_grid):
    return pltpu.CompilerParams(dimension_semantics=("arbitrary",) * n_grid,
                                vmem_limit_bytes=VMEM_LIMIT)


def _memkv_kernel(mem_ref, g_ref, w_ref, k_ref, v_ref, kb_ref, vb_ref, wb_scr):
    @pl.when(pl.program_id(0) == 0)
    def _():
        wb_scr[...] = w_ref[...].astype(BF16)

    h = _rms(mem_ref[0], g_ref[...]).astype(BF16)
    kv = _dot(h, wb_scr[...])
    k, v = kv[:, :D_MODEL], kv[:, D_MODEL:]
    for hd in range(N_HEADS):
        kb_ref[0, hd] = k[:, hd * HEAD_DIM:(hd + 1) * HEAD_DIM].astype(BF16)
        vb_ref[0, hd] = v[:, hd * HEAD_DIM:(hd + 1) * HEAD_DIM].astype(BF16)
    for hd in range(N_HEADS):
        for half in range(HEAD_DIM // LANES):
            rows = pl.ds(half * N_HEADS + hd, N_MEM, stride=CACHE_ROWS)
            cols = slice(hd * HEAD_DIM + half * LANES, hd * HEAD_DIM + (half + 1) * LANES)
            k_ref[0, rows, :] = k[:, cols]
            v_ref[0, rows, :] = v[:, cols]


def _memkv(mem, g_mem, w_mem_kv):
    b = mem.shape[0]
    blk = pl.BlockSpec((1, N_MEM, D_MODEL), lambda i: (i, 0, 0))
    rows = pl.BlockSpec((1, N_MEM * CACHE_ROWS, LANES), lambda i: (i, 0, 0))
    slabs = pl.BlockSpec((1, N_HEADS, N_MEM, HEAD_DIM), lambda i: (i, 0, 0, 0))
    k, v, kb, vb = pl.pallas_call(
        _memkv_kernel,
        grid=(b,),
        in_specs=[blk, _const_spec((1, D_MODEL)), _const_spec((D_MODEL, 2 * D_MODEL))],
        out_specs=[rows, rows, slabs, slabs],
        out_shape=[jax.ShapeDtypeStruct((b, N_MEM * CACHE_ROWS, LANES), F32)] * 2
        + [jax.ShapeDtypeStruct((b, N_HEADS, N_MEM, HEAD_DIM), BF16)] * 2,
        scratch_shapes=[pltpu.VMEM((D_MODEL, 2 * D_MODEL), BF16)],
        compiler_params=_params(1),
        name="memkv",
    )(mem, g_mem, w_mem_kv)
    return _cache_from_rows(k), _cache_from_rows(v), kb, vb


CONV_ROWS = 128


def _depthwise_conv(ext_ref, cw_ref, cb_ref, y_ref, t, lane_slices):
    first = HIST_PAD - HIST
    for j in lane_slices:
        cols = slice(j * LANES, (j + 1) * LANES)
        for r0 in range(0, t, CONV_ROWS):
            acc = jnp.broadcast_to(cb_ref[:, cols], (CONV_ROWS, LANES))
            for r in range(SUBLANES):
                rows = CONV_ROWS + (SUBLANES if r else 0)
                part = None
                for off in range(r, first + CONV_WIDTH, SUBLANES):
                    if off < first:
                        continue
                    w = off - first
                    base = r0 + off - r
                    term = ext_ref[base:base + rows, cols] * cw_ref[w:w + 1, cols]
                    part = term if part is None else part + term
                acc = acc + part[r:r + CONV_ROWS]
            y_ref[r0:r0 + CONV_ROWS, cols] = acc


def _prompt_mixer_kernel(chunk_decay,
                         x_ref, cos_ref, sin_ref, gmix_ref, win_ref, gn_ref, cw_ref, cb_ref,
                         lng_ref, lnb_ref, wco_ref, kb_ref, vb_ref, dmat_ref, qdec_ref, kdec_ref,
                         merged_ref, s_ref, cout_ref,
                         ext_scr, y_scr, conv_scr, part_scr, g1_scr):
    t = T_PROMPT

    @pl.when(pl.program_id(1) == 0)
    def _():
        s_ref[...] = jnp.zeros_like(s_ref)
        ext_scr[0:HIST_PAD, :] = jnp.zeros((HIST_PAD, D_MODEL), F32)

    h = _rms(x_ref[0], gmix_ref[...]).astype(BF16)

    def proj(off):
        return _dot(h, win_ref[off // HEAD_DIM])

    for j in range(D_MODEL // HEAD_DIM):
        cols = slice(j * HEAD_DIM, (j + 1) * HEAD_DIM)
        ext_scr[HIST_PAD:HIST_PAD + t, cols] = (
            proj(OFF_GLU_A + j * HEAD_DIM) * jax.nn.sigmoid(proj(OFF_GLU_G + j * HEAD_DIM)))

    cos, sin = cos_ref[...], sin_ref[...]
    conv_slices = D_MODEL // LANES // N_HEADS
    last = N_HEADS - 1
    for hd in range(N_HEADS):
        sl = slice(hd * HEAD_DIM, (hd + 1) * HEAD_DIM)
        _depthwise_conv(ext_scr, cw_ref, cb_ref, y_scr, t,
                        range(hd * conv_slices, (hd + 1) * conv_slices))
        qb = _rope(proj(OFF_Q + hd * HEAD_DIM), cos, sin).astype(BF16)
        kr = _rope(proj(OFF_K + hd * HEAD_DIM), cos, sin) * QK_SCALE
        vb = proj(OFF_V + hd * HEAD_DIM).astype(BF16)
        swish_g = _silu(proj(OFF_G + hd * HEAD_DIM))
        scores = _dot_nt(qb, kr.astype(BF16)) * dmat_ref[hd]
        qx = proj(OFF_XA + hd * HEAD_DIM).astype(BF16)
        s_old = s_ref[0, hd]
        o_cross = _dot(qb, s_old.astype(BF16)) * qdec_ref[hd]
        g0 = jax.nn.sigmoid(proj(OFF_GATE + hd * HEAD_DIM))
        o = _dot(scores.astype(BF16), vb) + o_cross
        s = _dot_nt(qx, kb_ref[0, hd]) * QK_SCALE
        g2 = jax.nn.sigmoid(proj(OFF_GATE + 2 * D_MODEL + hd * HEAD_DIM))
        kd_t = (kr * kdec_ref[hd]).T.astype(BF16)
        s_ref[0, hd] = s_old * chunk_decay[hd] + _dot(kd_t, vb)
        g1 = jax.nn.sigmoid(proj(OFF_GATE + D_MODEL + hd * HEAD_DIM))
        if hd == last:
            act = _silu(_standardize(y_scr[...]) * lng_ref[...] + lnb_ref[...])
            act = act.astype(BF16)
            for n in range(D_MODEL // HEAD_DIM):
                conv_scr[:, n * HEAD_DIM:(n + 1) * HEAD_DIM] = _dot(act, wco_ref[n])
            done = slice(0, hd * HEAD_DIM)
            merged_ref[0, :, done] = (
                part_scr[:, done] + g1_scr[:, done] * conv_scr[:, done]).astype(BF16)
        e = jnp.exp(s - jnp.max(s, axis=-1, keepdims=True))
        p = e / jnp.sum(e, axis=-1, keepdims=True)
        xa = _dot(p.astype(BF16), vb_ref[0, hd])
        ret = _standardize(o) * gn_ref[:, sl] * swish_g
        part = g0 * ret + g2 * xa
        if hd < last:
            part_scr[:, sl] = part
            g1_scr[:, sl] = g1
        else:
            merged_ref[0, :, sl] = (part + g1 * conv_scr[:, sl]).astype(BF16)

    cout_ref[0] = ext_scr[t + HIST_PAD - HIST:t + HIST_PAD, :]
    ext_scr[0:HIST_PAD, :] = ext_scr[t:t + HIST_PAD, :]


def _decay_tables(log_gamma, c):
    idx = jnp.arange(c, dtype=F32)
    diff = idx[:, None] - idx[None, :]
    causal = diff >= 0
    intra = jnp.where(causal[None],
                      jnp.exp(log_gamma[:, None, None] * jnp.where(causal, diff, 0.0)[None]), 0.0)
    q_decay = jnp.exp(log_gamma[:, None] * (idx + 1.0))
    k_decay = jnp.exp(log_gamma[:, None] * (c - 1.0 - idx))
    chunk_decay = jnp.exp(log_gamma * c)
    return intra, q_decay, k_decay, chunk_decay


def _log_gammas():
    return jnp.log1p(-jnp.exp2(-5.0 - jnp.arange(N_HEADS, dtype=F32)))


def _rope_tables(pos):
    inv = ROPE_BASE ** (-jnp.arange(0, HEAD_DIM, 2, dtype=F32) / HEAD_DIM)
    ang = pos.astype(F32)[:, None] * inv[None, :]
    return jnp.cos(ang), jnp.sin(ang)


def _prompt_mixer(x, kb, vb, w):
    b, l, _ = x.shape
    t = T_PROMPT
    n_chunks = l // t
    cos, sin = _rope_tables(jnp.arange(l))
    intra, q_decay, k_decay, chunk_decay = _decay_tables(_log_gammas(), t)
    qdec = jnp.broadcast_to(q_decay[:, :, None], (N_HEADS, t, HEAD_DIM))
    kdec = jnp.broadcast_to(k_decay[:, :, None], (N_HEADS, t, HEAD_DIM))

    def body(cd_ref, *refs):
        cd = tuple(cd_ref[i] for i in range(N_HEADS))
        _prompt_mixer_kernel(cd, *refs)

    tok = pl.BlockSpec((1, t, D_MODEL), lambda i, j: (i, j, 0))
    rope = pl.BlockSpec((t, HALF), lambda i, j: (j, 0))
    mem = pl.BlockSpec((1, N_HEADS, N_MEM, HEAD_DIM), lambda i, j: (i, 0, 0, 0))
    row = _const_spec((1, D_MODEL))
    table = _const_spec((N_HEADS, t, HEAD_DIM))
    return pl.pallas_call(
        body,
        grid=(b, n_chunks),
        in_specs=[pl.BlockSpec(memory_space=pltpu.SMEM),
                  tok, rope, rope, row, _const_spec((D_IN // HEAD_DIM, D_MODEL, HEAD_DIM)), row,
                  _const_spec((CONV_WIDTH, D_MODEL)), row, row, row,
                  _const_spec((D_MODEL // HEAD_DIM, D_MODEL, HEAD_DIM)), mem, mem,
                  table, table, table],
        out_specs=[tok,
                   pl.BlockSpec((1, N_HEADS, HEAD_DIM, HEAD_DIM), lambda i, j: (i, 0, 0, 0)),
                   pl.BlockSpec((1, HIST, D_MODEL), lambda i, j: (i, 0, 0))],
        out_shape=[jax.ShapeDtypeStruct((b, l, D_MODEL), BF16),
                   jax.ShapeDtypeStruct((b, N_HEADS, HEAD_DIM, HEAD_DIM), F32),
                   jax.ShapeDtypeStruct((b, HIST, D_MODEL), F32)],
        scratch_shapes=[pltpu.VMEM((HIST_PAD + t, D_MODEL), F32)]
        + [pltpu.VMEM((t, D_MODEL), F32)] * 4,
        compiler_params=_params(2),
        name="prompt_mixer",
    )(chunk_decay, x, cos, sin, w["g_mix"], w["w_in"], w["ret_gn_g"], w["conv_w"],
      w["conv_b"], w["conv_ln_g"], w["conv_ln_b"], w["w_conv_out"], kb, vb, intra, qdec, kdec)


def _ffn_kernel(x_ref, m_ref, wout_ref, gffn_ref, wup_ref, wdown_ref, gfin_ref, y_ref):
    groups = [slice(r, r + FFN_ROWS) for r in range(0, x_ref.shape[0], FFN_ROWS)]
    x1 = [x_ref[g, :] + _dot(m_ref[g, :].astype(BF16), wout_ref[...]) for g in groups]
    h2 = [_rms(v, gffn_ref[...]).astype(BF16) for v in x1]
    a = [jnp.square(jnp.maximum(_dot(v, wup_ref[...]), 0.0)).astype(BF16) for v in h2]
    for g, v, act in zip(groups, x1, a):
        y_ref[g, :] = _rms(v + _dot(act, wdown_ref[...]), gfin_ref[...])


def _ffn(x, merged, w):
    m = x.shape[0]
    tok = pl.BlockSpec((T_FFN, D_MODEL), lambda i: (i, 0))
    row = _const_spec((1, D_MODEL))
    return pl.pallas_call(
        _ffn_kernel,
        grid=(m // T_FFN,),
        in_specs=[tok, tok, _const_spec((D_MODEL, D_MODEL)), row, _const_spec((D_MODEL, D_FF)),
                  _const_spec((D_FF, D_MODEL)), row],
        out_specs=tok,
        out_shape=jax.ShapeDtypeStruct((m, D_MODEL), F32),
        compiler_params=_params(1),
        name="ffn",
    )(x, merged, w["w_out"], w["g_ffn"], w["w_up"], w["w_down"], w["g_final"])


SAMPLE_PROJ_COLS = 1024


def _sample_proj_kernel(x_ref, g_ref, w_ref, o_ref, wb_ref):
    h = _rms(x_ref[...], g_ref[...]).astype(BF16)
    wb = w_ref[...].astype(BF16)
    for q in range(SAMPLE_PROJ_COLS // HEAD_DIM):
        wb_ref[q] = wb[:, q * HEAD_DIM:(q + 1) * HEAD_DIM]
    o_ref[...] = _dot(h, wb)


def _sample_proj(x, g_mix, w_in):
    m = x.shape[0]
    slabs = SAMPLE_PROJ_COLS // HEAD_DIM
    return pl.pallas_call(
        _sample_proj_kernel,
        grid=(D_IN // SAMPLE_PROJ_COLS,),
        in_specs=[_const_spec((m, D_MODEL)), _const_spec((1, D_MODEL)),
                  pl.BlockSpec((D_MODEL, SAMPLE_PROJ_COLS), lambda j: (0, j))],
        out_specs=[pl.BlockSpec((m, SAMPLE_PROJ_COLS), lambda j: (0, j)),
                   pl.BlockSpec((slabs, D_MODEL, HEAD_DIM), lambda j: (j, 0, 0))],
        out_shape=[jax.ShapeDtypeStruct((m, D_IN), F32),
                   jax.ShapeDtypeStruct((D_IN // HEAD_DIM, D_MODEL, HEAD_DIM), BF16)],
        compiler_params=_params(1),
        name="sample_proj",
    )(x, g_mix, w_in)


def _head_rows(ref, bi, hd):
    halves = [ref[bi, pl.ds(half * N_HEADS + hd, N_MEM, stride=CACHE_ROWS), :]
              for half in range(HEAD_DIM // LANES)]
    return jnp.concatenate(halves, axis=-1)


SAMPLE_CONV_LANES = LANES


def _sample_conv_kernel(l, a_ref, g_ref, st_ref, cw_ref, cb_ref, lng_ref, lnb_ref, wco_ref,
                        conv_ref, st_out_ref, u_scr, y_scr):
    j = pl.program_id(0)
    nb = st_ref.shape[1]
    u_scr[...] = a_ref[...] * jax.nn.sigmoid(g_ref[...])

    def plane(k):
        return st_ref[k] if k < HIST else u_scr[pl.ds(k - HIST, nb, stride=l), :]

    for w in range(HIST):
        st_out_ref[w] = plane(w + l)
    for i in range(l):
        acc = jnp.broadcast_to(cb_ref[...], (nb, SAMPLE_CONV_LANES))
        for w in range(CONV_WIDTH):
            acc = acc + plane(i + w) * cw_ref[w:w + 1, :]
        y_scr[j, pl.ds(i, nb, stride=l), :] = acc

    @pl.when(j == pl.num_programs(0) - 1)
    def _():
        y = jnp.concatenate([y_scr[n] for n in range(D_MODEL // SAMPLE_CONV_LANES)], axis=-1)
        act = _silu(_standardize(y) * lng_ref[...] + lnb_ref[...]).astype(BF16)
        for n in range(D_MODEL // HEAD_DIM):
            conv_ref[:, n * HEAD_DIM:(n + 1) * HEAD_DIM] = _dot(act, wco_ref[n])


def _sample_conv(proj, state_t, w, l):
    m = proj.shape[0]
    nb = state_t.shape[1]
    lanes = SAMPLE_CONV_LANES
    glu_a = pl.BlockSpec((m, lanes), lambda j: (0, OFF_GLU_A // lanes + j))
    glu_g = pl.BlockSpec((m, lanes), lambda j: (0, OFF_GLU_G // lanes + j))
    st = pl.BlockSpec((HIST, nb, lanes), lambda j: (0, 0, j))
    row_blk = pl.BlockSpec((1, lanes), lambda j: (0, j))
    return pl.pallas_call(
        functools.partial(_sample_conv_kernel, l),
        grid=(D_MODEL // lanes,),
        in_specs=[glu_a, glu_g, st, pl.BlockSpec((CONV_WIDTH, lanes), lambda j: (0, j)), row_blk,
                  _const_spec((1, D_MODEL)), _const_spec((1, D_MODEL)),
                  _const_spec((D_MODEL // HEAD_DIM, D_MODEL, HEAD_DIM))],
        out_specs=[pl.BlockSpec((m, D_MODEL), lambda j: (0, 0)), st],
        out_shape=[jax.ShapeDtypeStruct((m, D_MODEL), F32),
                   jax.ShapeDtypeStruct(state_t.shape, F32)],
        scratch_shapes=[pltpu.VMEM((m, lanes), F32),
                        pltpu.VMEM((D_MODEL // lanes, m, lanes), F32)],
        compiler_params=_params(1),
        name="sample_conv",
    )(proj, proj, state_t, w["conv_w"], w["conv_b"], w["conv_ln_g"], w["conv_ln_b"],
      w["w_conv_out"])


def _sample_mixer_kernel(chunk_decay, l,
                         p_ref, conv_ref, cos_ref, sin_ref, s_ref, k_ref, v_ref,
                         gn_ref, dmat_ref, qdec_ref, kdec_ref,
                         merged_ref, sout_ref):
    rows = B_SAMPLE * l
    pad = 128
    row_id = lax.broadcasted_iota(jnp.int32, (rows, HEAD_DIM), 0)
    col_id = lax.broadcasted_iota(jnp.int32, (HEAD_DIM, pad), 1)
    row_of = [(row_id >= bi * l) & (row_id < (bi + 1) * l) for bi in range(B_SAMPLE)]
    col_of = [(col_id >= bi * l) & (col_id < (bi + 1) * l) for bi in range(B_SAMPLE)]

    def proj(off, n=HEAD_DIM):
        return p_ref[:, off:off + n]

    cos, sin = cos_ref[...], sin_ref[...]
    zeros_pad = jnp.zeros((pad - rows, HEAD_DIM), F32)

    def first_stage(hd):
        qb = _rope(proj(OFF_Q + hd * HEAD_DIM), cos, sin).astype(BF16)
        kr = _rope(proj(OFF_K + hd * HEAD_DIM), cos, sin) * QK_SCALE
        v = proj(OFF_V + hd * HEAD_DIM)
        kb_pad = jnp.concatenate([kr, zeros_pad], axis=0).astype(BF16)
        vb_pad = jnp.concatenate([v, zeros_pad], axis=0).astype(BF16)
        scores = _dot_nt(qb, kb_pad) * dmat_ref[hd]
        kd_t = jnp.concatenate([kr * kdec_ref[hd], zeros_pad], axis=0).T
        qx = proj(OFF_XA + hd * HEAD_DIM).astype(BF16)
        o_cross = jnp.zeros((rows, HEAD_DIM), F32)
        s = jnp.zeros((rows, N_MEM), F32)
        for bi in range(B_SAMPLE):
            s_old = s_ref[bi, hd]
            o_cross = jnp.where(row_of[bi], _dot(qb, s_old.astype(BF16)), o_cross)
            kd_bi = jnp.where(col_of[bi], kd_t, 0.0).astype(BF16)
            sout_ref[bi, hd] = s_old * chunk_decay[hd] + _dot(kd_bi, vb_pad)
            s = jnp.where(row_of[bi], _dot_nt(qx, _head_rows(k_ref, bi, hd).astype(BF16)), s)
        return scores, vb_pad, o_cross, s * QK_SCALE

    def second_stage(hd, scores, vb_pad, o_cross, s):
        sl = slice(hd * HEAD_DIM, (hd + 1) * HEAD_DIM)
        o = _dot(scores.astype(BF16), vb_pad) + o_cross * qdec_ref[hd]
        ret = _standardize(o) * gn_ref[:, sl] * _silu(proj(OFF_G + hd * HEAD_DIM))
        e = jnp.exp(s - jnp.max(s, axis=-1, keepdims=True))
        pb = (e / jnp.sum(e, axis=-1, keepdims=True)).astype(BF16)
        xa = jnp.zeros((rows, HEAD_DIM), F32)
        for bi in range(B_SAMPLE):
            xa = jnp.where(row_of[bi], _dot(pb, _head_rows(v_ref, bi, hd).astype(BF16)), xa)
        g0 = jax.nn.sigmoid(proj(OFF_GATE + hd * HEAD_DIM))
        g2 = jax.nn.sigmoid(proj(OFF_GATE + 2 * D_MODEL + hd * HEAD_DIM))
        return g0 * ret + g2 * xa

    parts = []
    staged = first_stage(0)
    for hd in range(N_HEADS):
        following = first_stage(hd + 1) if hd + 1 < N_HEADS else None
        parts.append(second_stage(hd, *staged))
        staged = following
    g1 = jax.nn.sigmoid(proj(OFF_GATE + D_MODEL, D_MODEL))
    merged_ref[...] = jnp.concatenate(parts, axis=-1) + g1 * conv_ref[...]


def _ffn_with_sample_mixer(x, merged, w, proj, conv, state_ret, cache_k, cache_v, l):
    m = x.shape[0]
    nb = state_ret.shape[0]
    steps = nb // B_SAMPLE
    t_rows = m // steps
    assert steps * B_SAMPLE == nb and steps * t_rows == m and t_rows % FFN_ROWS == 0
    rows = B_SAMPLE * l
    pad = 128
    cos, sin = _rope_tables(PAST_LEN + jnp.arange(l))
    cos, sin = jnp.tile(cos, (B_SAMPLE, 1)), jnp.tile(sin, (B_SAMPLE, 1))
    intra, q_decay, k_decay, chunk_decay = _decay_tables(_log_gammas(), l)
    same_req = (jnp.arange(rows)[:, None] // l) == (jnp.arange(pad)[None, :] // l)
    dmat = jnp.where(same_req[None], jnp.pad(jnp.tile(intra, (1, B_SAMPLE, B_SAMPLE)),
                                              ((0, 0), (0, 0), (0, pad - rows))), 0.0)
    qdec = jnp.broadcast_to(jnp.tile(q_decay, (1, B_SAMPLE))[:, :, None], (N_HEADS, rows, HEAD_DIM))
    kdec = jnp.broadcast_to(jnp.tile(k_decay, (1, B_SAMPLE))[:, :, None], (N_HEADS, rows, HEAD_DIM))

    n_ffn_in = 7

    def body(cd_ref, *refs):
        cd = tuple(cd_ref[i] for i in range(N_HEADS))
        ffn_in, rest = refs[:n_ffn_in], refs[n_ffn_in:]
        mixer_in, (y_ref, merged_ref, sout_ref) = rest[:-3], rest[-3:]
        _ffn_kernel(*ffn_in, y_ref)
        _sample_mixer_kernel(cd, l, *mixer_in, merged_ref, sout_ref)

    row = _const_spec((1, D_MODEL))
    ptok = pl.BlockSpec((t_rows, D_MODEL), lambda i: (i, 0))
    state = pl.BlockSpec((B_SAMPLE, N_HEADS, HEAD_DIM, HEAD_DIM), lambda i: (i, 0, 0, 0))
    tok = pl.BlockSpec((rows, D_MODEL), lambda i: (i, 0))
    cache = pl.BlockSpec((B_SAMPLE, N_MEM * CACHE_ROWS, LANES), lambda i: (i, 0, 0))
    return pl.pallas_call(
        body,
        grid=(steps,),
        in_specs=[pl.BlockSpec(memory_space=pltpu.SMEM),
                  ptok, ptok, _const_spec((D_MODEL, D_MODEL)), row, _const_spec((D_MODEL, D_FF)),
                  _const_spec((D_FF, D_MODEL)), row,
                  pl.BlockSpec((rows, D_IN), lambda i: (i, 0)), tok,
                  _const_spec((rows, HALF)), _const_spec((rows, HALF)),
                  state, cache, cache, row,
                  _const_spec((N_HEADS, rows, pad)),
                  _const_spec((N_HEADS, rows, HEAD_DIM)), _const_spec((N_HEADS, rows, HEAD_DIM))],
        out_specs=[ptok, tok, state],
        out_shape=[jax.ShapeDtypeStruct((m, D_MODEL), F32),
                   jax.ShapeDtypeStruct((nb * l, D_MODEL), F32),
                   jax.ShapeDtypeStruct(state_ret.shape, F32)],
        compiler_params=_params(1),
        name="ffn_sample_mixer",
    )(chunk_decay, x, merged, w["w_out"], w["g_ffn"], w["w_up"], w["w_down"], w["g_final"],
      proj, conv, cos, sin, state_ret, cache_k, cache_v, w["ret_gn_g"], dmat, qdec, kdec)


def kernel(x_prompt, x_sample, mem_prompt, state_ret, state_conv, cache_mem_k, cache_mem_v, g_mix, w_in, ret_gn_g, conv_w, conv_b, conv_ln_g, conv_ln_b, w_conv_out, w_out, g_ffn, w_up, w_down, g_mem, w_mem_kv, g_final):
    assert state_ret.shape[0] == 1, "one layer"
    b_p, l_p, _ = x_prompt.shape
    b_s, l_s, _ = x_sample.shape
    w = {
        "g_mix": g_mix[0][None], "ret_gn_g": ret_gn_g[0][None],
        "conv_w": conv_w[0], "conv_b": conv_b[0][None], "conv_ln_g": conv_ln_g[0][None],
        "conv_ln_b": conv_ln_b[0][None], "w_conv_out": _column_slabs(w_conv_out[0].astype(BF16)),
        "w_out": w_out[0].astype(BF16), "g_ffn": g_ffn[0][None], "w_up": w_up[0].astype(BF16),
        "w_down": w_down[0].astype(BF16), "g_final": g_final[None],
    }

    xs = x_sample.reshape(b_s * l_s, D_MODEL)
    proj_s, w["w_in"] = _sample_proj(xs, w["g_mix"], w_in[0])
    conv_out_s, conv_s = _sample_conv(proj_s, jnp.transpose(state_conv[0], (1, 0, 2)), w, l_s)
    conv_s = jnp.transpose(conv_s, (1, 0, 2))

    mk, mv, mkb, mvb = _memkv(mem_prompt, g_mem[0][None], w_mem_kv[0])
    merged_p, ret_p, conv_p = _prompt_mixer(x_prompt, mkb, mvb, w)
    y_prompt, merged_s, ret_s = _ffn_with_sample_mixer(
        x_prompt.reshape(b_p * l_p, D_MODEL), merged_p.reshape(b_p * l_p, D_MODEL), w,
        proj_s, conv_out_s, state_ret[0],
        _cache_rows(cache_mem_k[0]), _cache_rows(cache_mem_v[0]), l_s)
    y_sample = _ffn(xs, merged_s, w)

    return (y_prompt.reshape(b_p, l_p, D_MODEL), y_sample.reshape(b_s, l_s, D_MODEL),
            ret_p[None], conv_p[None], mk[None], mv[None], ret_s[None], conv_s[None])
```

```python
import functools

import jax
import jax.numpy as jnp
from jax import lax
from jax.experimental import pallas as pl
from jax.experimental.pallas import tpu as pltpu

F32 = jnp.float32
BF16 = jnp.bfloat16

D_MODEL = 1024
N_HEADS = 4
HEAD_DIM = 256
HALF = HEAD_DIM // 2
D_FF = 4 * D_MODEL
CONV_WIDTH = 31
HIST = CONV_WIDTH - 1
N_MEM = 256
PAST_LEN = 16384
ROPE_BASE = 10000.0
EPS = 1e-6
QK_SCALE = HEAD_DIM ** -0.5

OFF_Q, OFF_K, OFF_V, OFF_G = 0, 1024, 2048, 3072
OFF_GLU_A, OFF_GLU_G, OFF_XA, OFF_GATE = 4096, 5120, 6144, 7168
D_IN = 10240

SUBLANES = 8
HIST_PAD = 32
T_PROMPT = 256
T_FFN = 512
FFN_ROWS = 256
B_SAMPLE = 2
VMEM_LIMIT = 56 * 1024 * 1024

LANES = 128
CACHE_ROWS = N_HEADS * HEAD_DIM // LANES

NT_DIMS = (((1,), (1,)), ((), ()))


def _cache_rows(c):
    b = c.shape[0]
    c = c.reshape(b, N_MEM, N_HEADS, HEAD_DIM // LANES, LANES).transpose(0, 1, 3, 2, 4)
    return c.reshape(b, N_MEM * CACHE_ROWS, LANES)


def _column_slabs(w):
    k, n = w.shape
    return w.reshape(k, n // HEAD_DIM, HEAD_DIM).transpose(1, 0, 2)


def _cache_from_rows(r):
    b = r.shape[0]
    r = r.reshape(b, N_MEM, HEAD_DIM // LANES, N_HEADS, LANES).transpose(0, 1, 3, 2, 4)
    return r.reshape(b, N_MEM, N_HEADS, HEAD_DIM)


def _dot(a, b):
    return jnp.dot(a, b, preferred_element_type=F32)


def _dot_nt(a, b):
    return lax.dot_general(a, b, NT_DIMS, preferred_element_type=F32)


def _rms(x, g):
    return x * lax.rsqrt(jnp.mean(x * x, axis=-1, keepdims=True) + EPS) * g


def _standardize(x):
    mu = jnp.mean(x, axis=-1, keepdims=True)
    xc = x - mu
    return xc * lax.rsqrt(jnp.mean(xc * xc, axis=-1, keepdims=True) + EPS)


def _silu(x):
    return x * jax.nn.sigmoid(x)


def _rope(x, cos, sin):
    x1, x2 = x[:, :HALF], x[:, HALF:]
    return jnp.concatenate([x1 * cos - x2 * sin, x2 * cos + x1 * sin], axis=-1)


def _const_spec(shape):
    return pl.BlockSpec(shape, lambda *_: (0,) * len(shape), pipeline_mode=pl.Buffered(1))


def _params(n_grid):
    return pltpu.CompilerParams(dimension_semantics=("arbitrary",) * n_grid,
                                vmem_limit_bytes=VMEM_LIMIT)


def _memkv_kernel(mem_ref, g_ref, w_ref, k_ref, v_ref, kb_ref, vb_ref, wb_scr):
    @pl.when(pl.program_id(0) == 0)
    def _():
        wb_scr[...] = w_ref[...].astype(BF16)

    h = _rms(mem_ref[0], g_ref[...]).astype(BF16)
    kv = _dot(h, wb_scr[...])
    k, v = kv[:, :D_MODEL], kv[:, D_MODEL:]
    for hd in range(N_HEADS):
        kb_ref[0, hd] = k[:, hd * HEAD_DIM:(hd + 1) * HEAD_DIM].astype(BF16)
        vb_ref[0, hd] = v[:, hd * HEAD_DIM:(hd + 1) * HEAD_DIM].astype(BF16)
    for hd in range(N_HEADS):
        for half in range(HEAD_DIM // LANES):
            rows = pl.ds(half * N_HEADS + hd, N_MEM, stride=CACHE_ROWS)
            cols = slice(hd * HEAD_DIM + half * LANES, hd * HEAD_DIM + (half + 1) * LANES)
            k_ref[0, rows, :] = k[:, cols]
            v_ref[0, rows, :] = v[:, cols]


def _memkv(mem, g_mem, w_mem_kv):
    b = mem.shape[0]
    blk = pl.BlockSpec((1, N_MEM, D_MODEL), lambda i: (i, 0, 0))
    rows = pl.BlockSpec((1, N_MEM * CACHE_ROWS, LANES), lambda i: (i, 0, 0))
    slabs = pl.BlockSpec((1, N_HEADS, N_MEM, HEAD_DIM), lambda i: (i, 0, 0, 0))
    k, v, kb, vb = pl.pallas_call(
        _memkv_kernel,
        grid=(b,),
        in_specs=[blk, _const_spec((1, D_MODEL)), _const_spec((D_MODEL, 2 * D_MODEL))],
        out_specs=[rows, rows, slabs, slabs],
        out_shape=[jax.ShapeDtypeStruct((b, N_MEM * CACHE_ROWS, LANES), F32)] * 2
        + [jax.ShapeDtypeStruct((b, N_HEADS, N_MEM, HEAD_DIM), BF16)] * 2,
        scratch_shapes=[pltpu.VMEM((D_MODEL, 2 * D_MODEL), BF16)],
        compiler_params=_params(1),
        name="memkv",
    )(mem, g_mem, w_mem_kv)
    return _cache_from_rows(k), _cache_from_rows(v), kb, vb


CONV_ROWS = 128


def _depthwise_conv(ext_ref, cw_ref, cb_ref, y_ref, t, lane_slices):
    first = HIST_PAD - HIST
    for j in lane_slices:
        cols = slice(j * LANES, (j + 1) * LANES)
        for r0 in range(0, t, CONV_ROWS):
            acc = jnp.broadcast_to(cb_ref[:, cols], (CONV_ROWS, LANES))
            for r in range(SUBLANES):
                rows = CONV_ROWS + (SUBLANES if r else 0)
                part = None
                for off in range(r, first + CONV_WIDTH, SUBLANES):
                    if off < first:
                        continue
                    w = off - first
                    base = r0 + off - r
                    term = ext_ref[base:base + rows, cols] * cw_ref[w:w + 1, cols]
                    part = term if part is None else part + term
                acc = acc + part[r:r + CONV_ROWS]
            y_ref[r0:r0 + CONV_ROWS, cols] = acc


def _prompt_mixer_kernel(chunk_decay,
                         x_ref, cos_ref, sin_ref, gmix_ref, win_ref, gn_ref, cw_ref, cb_ref,
                         lng_ref, lnb_ref, wco_ref, kb_ref, vb_ref, dmat_ref, qdec_ref, kdec_ref,
                         wout_f_ref, wup_f_ref, wdown_f_ref,
                         merged_ref, s_ref, cout_ref, wout_b_ref, wup_b_ref, wdown_b_ref,
                         ext_scr, y_scr, conv_scr, part_scr, g1_scr):
    t = T_PROMPT

    for f_ref, b_ref in ((wout_f_ref, wout_b_ref), (wup_f_ref, wup_b_ref),
                         (wdown_f_ref, wdown_b_ref)):
        b_ref[...] = f_ref[...].astype(BF16)

    @pl.when(pl.program_id(1) == 0)
    def _():
        s_ref[...] = jnp.zeros_like(s_ref)
        ext_scr[0:HIST_PAD, :] = jnp.zeros((HIST_PAD, D_MODEL), F32)

    h = _rms(x_ref[0], gmix_ref[...]).astype(BF16)

    def proj(off):
        return _dot(h, win_ref[off // HEAD_DIM])

    for j in range(D_MODEL // HEAD_DIM):
        cols = slice(j * HEAD_DIM, (j + 1) * HEAD_DIM)
        ext_scr[HIST_PAD:HIST_PAD + t, cols] = (
            proj(OFF_GLU_A + j * HEAD_DIM) * jax.nn.sigmoid(proj(OFF_GLU_G + j * HEAD_DIM)))

    cos, sin = cos_ref[...], sin_ref[...]
    conv_slices = D_MODEL // LANES // N_HEADS
    last = N_HEADS - 1
    for hd in range(N_HEADS):
        sl = slice(hd * HEAD_DIM, (hd + 1) * HEAD_DIM)
        _depthwise_conv(ext_scr, cw_ref, cb_ref, y_scr, t,
                        range(hd * conv_slices, (hd + 1) * conv_slices))
        qb = _rope(proj(OFF_Q + hd * HEAD_DIM), cos, sin).astype(BF16)
        kr = _rope(proj(OFF_K + hd * HEAD_DIM), cos, sin) * QK_SCALE
        vb = proj(OFF_V + hd * HEAD_DIM).astype(BF16)
        swish_g = _silu(proj(OFF_G + hd * HEAD_DIM))
        scores = _dot_nt(qb, kr.astype(BF16)) * dmat_ref[hd]
        qx = proj(OFF_XA + hd * HEAD_DIM).astype(BF16)
        s_old = s_ref[0, hd]
        o_cross = _dot(qb, s_old.astype(BF16)) * qdec_ref[hd]
        g0 = jax.nn.sigmoid(proj(OFF_GATE + hd * HEAD_DIM))
        o = _dot(scores.astype(BF16), vb) + o_cross
        s = _dot_nt(qx, kb_ref[0, hd]) * QK_SCALE
        g2 = jax.nn.sigmoid(proj(OFF_GATE + 2 * D_MODEL + hd * HEAD_DIM))
        kd_t = (kr * kdec_ref[hd]).T.astype(BF16)
        s_ref[0, hd] = s_old * chunk_decay[hd] + _dot(kd_t, vb)
        g1 = jax.nn.sigmoid(proj(OFF_GATE + D_MODEL + hd * HEAD_DIM))
        if hd == last:
            act = _silu(_standardize(y_scr[...]) * lng_ref[...] + lnb_ref[...])
            act = act.astype(BF16)
            for n in range(D_MODEL // HEAD_DIM):
                conv_scr[:, n * HEAD_DIM:(n + 1) * HEAD_DIM] = _dot(act, wco_ref[n])
            done = slice(0, hd * HEAD_DIM)
            merged_ref[0, :, done] = (
                part_scr[:, done] + g1_scr[:, done] * conv_scr[:, done]).astype(BF16)
        e = jnp.exp(s - jnp.max(s, axis=-1, keepdims=True))
        p = e / jnp.sum(e, axis=-1, keepdims=True)
        xa = _dot(p.astype(BF16), vb_ref[0, hd])
        ret = _standardize(o) * gn_ref[:, sl] * swish_g
        part = g0 * ret + g2 * xa
        if hd < last:
            part_scr[:, sl] = part
            g1_scr[:, sl] = g1
        else:
            merged_ref[0, :, sl] = (part + g1 * conv_scr[:, sl]).astype(BF16)

    cout_ref[0] = ext_scr[t + HIST_PAD - HIST:t + HIST_PAD, :]
    ext_scr[0:HIST_PAD, :] = ext_scr[t:t + HIST_PAD, :]


def _decay_tables(log_gamma, c):
    idx = jnp.arange(c, dtype=F32)
    diff = idx[:, None] - idx[None, :]
    causal = diff >= 0
    intra = jnp.where(causal[None],
                      jnp.exp(log_gamma[:, None, None] * jnp.where(causal, diff, 0.0)[None]), 0.0)
    q_decay = jnp.exp(log_gamma[:, None] * (idx + 1.0))
    k_decay = jnp.exp(log_gamma[:, None] * (c - 1.0 - idx))
    chunk_decay = jnp.exp(log_gamma * c)
    return intra, q_decay, k_decay, chunk_decay


def _log_gammas():
    return jnp.log1p(-jnp.exp2(-5.0 - jnp.arange(N_HEADS, dtype=F32)))


def _rope_tables(pos):
    inv = ROPE_BASE ** (-jnp.arange(0, HEAD_DIM, 2, dtype=F32) / HEAD_DIM)
    ang = pos.astype(F32)[:, None] * inv[None, :]
    return jnp.cos(ang), jnp.sin(ang)


def _prompt_mixer(x, kb, vb, w, mlp_weights):
    b, l, _ = x.shape
    t = T_PROMPT
    n_chunks = l // t
    steps = b * n_chunks

    def row_block(wt):
        assert wt.shape[0] % (steps * 2 * SUBLANES) == 0
        return pl.BlockSpec((wt.shape[0] // steps, wt.shape[1]), lambda i, j: (i * n_chunks + j, 0))

    mlp_blocks = [row_block(wt) for wt in mlp_weights]
    cos, sin = _rope_tables(jnp.arange(l))
    intra, q_decay, k_decay, chunk_decay = _decay_tables(_log_gammas(), t)
    qdec = jnp.broadcast_to(q_decay[:, :, None], (N_HEADS, t, HEAD_DIM))
    kdec = jnp.broadcast_to(k_decay[:, :, None], (N_HEADS, t, HEAD_DIM))

    def body(cd_ref, *refs):
        cd = tuple(cd_ref[i] for i in range(N_HEADS))
        _prompt_mixer_kernel(cd, *refs)

    tok = pl.BlockSpec((1, t, D_MODEL), lambda i, j: (i, j, 0))
    rope = pl.BlockSpec((t, HALF), lambda i, j: (j, 0))
    mem = pl.BlockSpec((1, N_HEADS, N_MEM, HEAD_DIM), lambda i, j: (i, 0, 0, 0))
    row = _const_spec((1, D_MODEL))
    table = _const_spec((N_HEADS, t, HEAD_DIM))
    return pl.pallas_call(
        body,
        grid=(b, n_chunks),
        in_specs=[pl.BlockSpec(memory_space=pltpu.SMEM),
                  tok, rope, rope, row, _const_spec((D_IN // HEAD_DIM, D_MODEL, HEAD_DIM)), row,
                  _const_spec((CONV_WIDTH, D_MODEL)), row, row, row,
                  _const_spec((D_MODEL // HEAD_DIM, D_MODEL, HEAD_DIM)), mem, mem,
                  table, table, table] + mlp_blocks,
        out_specs=[tok,
                   pl.BlockSpec((1, N_HEADS, HEAD_DIM, HEAD_DIM), lambda i, j: (i, 0, 0, 0)),
                   pl.BlockSpec((1, HIST, D_MODEL), lambda i, j: (i, 0, 0))] + mlp_blocks,
        out_shape=[jax.ShapeDtypeStruct((b, l, D_MODEL), BF16),
                   jax.ShapeDtypeStruct((b, N_HEADS, HEAD_DIM, HEAD_DIM), F32),
                   jax.ShapeDtypeStruct((b, HIST, D_MODEL), F32)]
        + [jax.ShapeDtypeStruct(wt.shape, BF16) for wt in mlp_weights],
        scratch_shapes=[pltpu.VMEM((HIST_PAD + t, D_MODEL), F32)]
        + [pltpu.VMEM((t, D_MODEL), F32)] * 4,
        compiler_params=_params(2),
        name="prompt_mixer",
    )(chunk_decay, x, cos, sin, w["g_mix"], w["w_in"], w["ret_gn_g"], w["conv_w"],
      w["conv_b"], w["conv_ln_g"], w["conv_ln_b"], w["w_conv_out"], kb, vb, intra, qdec, kdec,
      *mlp_weights)


def _ffn_kernel(x_ref, m_ref, wout_ref, gffn_ref, wup_ref, wdown_ref, gfin_ref, y_ref):
    groups = [slice(r, r + FFN_ROWS) for r in range(0, x_ref.shape[0], FFN_ROWS)]
    x1 = [x_ref[g, :] + _dot(m_ref[g, :].astype(BF16), wout_ref[...]) for g in groups]
    h2 = [_rms(v, gffn_ref[...]).astype(BF16) for v in x1]
    a = [jnp.square(jnp.maximum(_dot(v, wup_ref[...]), 0.0)).astype(BF16) for v in h2]
    for g, v, act in zip(groups, x1, a):
        y_ref[g, :] = _rms(v + _dot(act, wdown_ref[...]), gfin_ref[...])


def _ffn(x, merged, w):
    m = x.shape[0]
    tok = pl.BlockSpec((T_FFN, D_MODEL), lambda i: (i, 0))
    row = _const_spec((1, D_MODEL))
    return pl.pallas_call(
        _ffn_kernel,
        grid=(m // T_FFN,),
        in_specs=[tok, tok, _const_spec((D_MODEL, D_MODEL)), row, _const_spec((D_MODEL, D_FF)),
                  _const_spec((D_FF, D_MODEL)), row],
        out_specs=tok,
        out_shape=jax.ShapeDtypeStruct((m, D_MODEL), F32),
        compiler_params=_params(1),
        name="ffn",
    )(x, merged, w["w_out"], w["g_ffn"], w["w_up"], w["w_down"], w["g_final"])


SAMPLE_PROJ_COLS = 1024


def _sample_proj_kernel(x_ref, g_ref, w_ref, o_ref, wb_ref):
    h = _rms(x_ref[...], g_ref[...]).astype(BF16)
    wb = w_ref[...].astype(BF16)
    for q in range(SAMPLE_PROJ_COLS // HEAD_DIM):
        wb_ref[q] = wb[:, q * HEAD_DIM:(q + 1) * HEAD_DIM]
    o_ref[...] = _dot(h, wb)


def _sample_proj(x, g_mix, w_in):
    m = x.shape[0]
    slabs = SAMPLE_PROJ_COLS // HEAD_DIM
    return pl.pallas_call(
        _sample_proj_kernel,
        grid=(D_IN // SAMPLE_PROJ_COLS,),
        in_specs=[_const_spec((m, D_MODEL)), _const_spec((1, D_MODEL)),
                  pl.BlockSpec((D_MODEL, SAMPLE_PROJ_COLS), lambda j: (0, j))],
        out_specs=[pl.BlockSpec((m, SAMPLE_PROJ_COLS), lambda j: (0, j)),
                   pl.BlockSpec((slabs, D_MODEL, HEAD_DIM), lambda j: (j, 0, 0))],
        out_shape=[jax.ShapeDtypeStruct((m, D_IN), F32),
                   jax.ShapeDtypeStruct((D_IN // HEAD_DIM, D_MODEL, HEAD_DIM), BF16)],
        compiler_params=_params(1),
        name="sample_proj",
    )(x, g_mix, w_in)


def _head_rows(ref, bi, hd):
    halves = [ref[bi, pl.ds(half * N_HEADS + hd, N_MEM, stride=CACHE_ROWS), :]
              for half in range(HEAD_DIM // LANES)]
    return jnp.concatenate(halves, axis=-1)


SAMPLE_CONV_LANES = LANES


def _sample_conv_kernel(l, a_ref, g_ref, st_ref, cw_ref, cb_ref, lng_ref, lnb_ref, wco_ref,
                        conv_ref, st_out_ref, u_scr, y_scr):
    j = pl.program_id(0)
    nb = st_ref.shape[1]
    u_scr[...] = a_ref[...] * jax.nn.sigmoid(g_ref[...])

    def plane(k):
        return st_ref[k] if k < HIST else u_scr[pl.ds(k - HIST, nb, stride=l), :]

    for w in range(HIST):
        st_out_ref[w] = plane(w + l)
    for i in range(l):
        acc = jnp.broadcast_to(cb_ref[...], (nb, SAMPLE_CONV_LANES))
        for w in range(CONV_WIDTH):
            acc = acc + plane(i + w) * cw_ref[w:w + 1, :]
        y_scr[j, pl.ds(i, nb, stride=l), :] = acc

    @pl.when(j == pl.num_programs(0) - 1)
    def _():
        y = jnp.concatenate([y_scr[n] for n in range(D_MODEL // SAMPLE_CONV_LANES)], axis=-1)
        act = _silu(_standardize(y) * lng_ref[...] + lnb_ref[...]).astype(BF16)
        for n in range(D_MODEL // HEAD_DIM):
            conv_ref[:, n * HEAD_DIM:(n + 1) * HEAD_DIM] = _dot(act, wco_ref[n])


def _sample_conv(proj, state_t, w, l):
    m = proj.shape[0]
    nb = state_t.shape[1]
    lanes = SAMPLE_CONV_LANES
    glu_a = pl.BlockSpec((m, lanes), lambda j: (0, OFF_GLU_A // lanes + j))
    glu_g = pl.BlockSpec((m, lanes), lambda j: (0, OFF_GLU_G // lanes + j))
    st = pl.BlockSpec((HIST, nb, lanes), lambda j: (0, 0, j))
    row_blk = pl.BlockSpec((1, lanes), lambda j: (0, j))
    return pl.pallas_call(
        functools.partial(_sample_conv_kernel, l),
        grid=(D_MODEL // lanes,),
        in_specs=[glu_a, glu_g, st, pl.BlockSpec((CONV_WIDTH, lanes), lambda j: (0, j)), row_blk,
                  _const_spec((1, D_MODEL)), _const_spec((1, D_MODEL)),
                  _const_spec((D_MODEL // HEAD_DIM, D_MODEL, HEAD_DIM))],
        out_specs=[pl.BlockSpec((m, D_MODEL), lambda j: (0, 0)), st],
        out_shape=[jax.ShapeDtypeStruct((m, D_MODEL), F32),
                   jax.ShapeDtypeStruct(state_t.shape, F32)],
        scratch_shapes=[pltpu.VMEM((m, lanes), F32),
                        pltpu.VMEM((D_MODEL // lanes, m, lanes), F32)],
        compiler_params=_params(1),
        name="sample_conv",
    )(proj, proj, state_t, w["conv_w"], w["conv_b"], w["conv_ln_g"], w["conv_ln_b"],
      w["w_conv_out"])


def _sample_mixer_kernel(chunk_decay, l,
                         p_ref, conv_ref, cos_ref, sin_ref, s_ref, k_ref, v_ref,
                         gn_ref, dmat_ref, qdec_ref, kdec_ref,
                         merged_ref, sout_ref):
    rows = B_SAMPLE * l
    pad = 128
    row_id = lax.broadcasted_iota(jnp.int32, (rows, HEAD_DIM), 0)
    col_id = lax.broadcasted_iota(jnp.int32, (HEAD_DIM, pad), 1)
    row_of = [(row_id >= bi * l) & (row_id < (bi + 1) * l) for bi in range(B_SAMPLE)]
    col_of = [(col_id >= bi * l) & (col_id < (bi + 1) * l) for bi in range(B_SAMPLE)]

    def proj(off, n=HEAD_DIM):
        return p_ref[:, off:off + n]

    cos, sin = cos_ref[...], sin_ref[...]
    zeros_pad = jnp.zeros((pad - rows, HEAD_DIM), F32)

    def first_stage(hd):
        qb = _rope(proj(OFF_Q + hd * HEAD_DIM), cos, sin).astype(BF16)
        kr = _rope(proj(OFF_K + hd * HEAD_DIM), cos, sin) * QK_SCALE
        v = proj(OFF_V + hd * HEAD_DIM)
        kb_pad = jnp.concatenate([kr, zeros_pad], axis=0).astype(BF16)
        vb_pad = jnp.concatenate([v, zeros_pad], axis=0).astype(BF16)
        scores = _dot_nt(qb, kb_pad) * dmat_ref[hd]
        kd_t = jnp.concatenate([kr * kdec_ref[hd], zeros_pad], axis=0).T
        qx = proj(OFF_XA + hd * HEAD_DIM).astype(BF16)
        o_cross = jnp.zeros((rows, HEAD_DIM), F32)
        s = jnp.zeros((rows, N_MEM), F32)
        for bi in range(B_SAMPLE):
            s_old = s_ref[bi, hd]
            o_cross = jnp.where(row_of[bi], _dot(qb, s_old.astype(BF16)), o_cross)
            kd_bi = jnp.where(col_of[bi], kd_t, 0.0).astype(BF16)
            sout_ref[bi, hd] = s_old * chunk_decay[hd] + _dot(kd_bi, vb_pad)
            s = jnp.where(row_of[bi], _dot_nt(qx, _head_rows(k_ref, bi, hd).astype(BF16)), s)
        return scores, vb_pad, o_cross, s * QK_SCALE

    def second_stage(hd, scores, vb_pad, o_cross, s):
        sl = slice(hd * HEAD_DIM, (hd + 1) * HEAD_DIM)
        o = _dot(scores.astype(BF16), vb_pad) + o_cross * qdec_ref[hd]
        ret = _standardize(o) * gn_ref[:, sl] * _silu(proj(OFF_G + hd * HEAD_DIM))
        e = jnp.exp(s - jnp.max(s, axis=-1, keepdims=True))
        pb = (e / jnp.sum(e, axis=-1, keepdims=True)).astype(BF16)
        xa = jnp.zeros((rows, HEAD_DIM), F32)
        for bi in range(B_SAMPLE):
            xa = jnp.where(row_of[bi], _dot(pb, _head_rows(v_ref, bi, hd).astype(BF16)), xa)
        g0 = jax.nn.sigmoid(proj(OFF_GATE + hd * HEAD_DIM))
        g2 = jax.nn.sigmoid(proj(OFF_GATE + 2 * D_MODEL + hd * HEAD_DIM))
        return g0 * ret + g2 * xa

    parts = []
    staged = first_stage(0)
    for hd in range(N_HEADS):
        following = first_stage(hd + 1) if hd + 1 < N_HEADS else None
        parts.append(second_stage(hd, *staged))
        staged = following
    g1 = jax.nn.sigmoid(proj(OFF_GATE + D_MODEL, D_MODEL))
    merged_ref[...] = jnp.concatenate(parts, axis=-1) + g1 * conv_ref[...]


def _ffn_with_sample_mixer(x, merged, w, proj, conv, state_ret, cache_k, cache_v, l):
    m = x.shape[0]
    nb = state_ret.shape[0]
    steps = nb // B_SAMPLE
    t_rows = m // steps
    assert steps * B_SAMPLE == nb and steps * t_rows == m and t_rows % FFN_ROWS == 0
    rows = B_SAMPLE * l
    pad = 128
    cos, sin = _rope_tables(PAST_LEN + jnp.arange(l))
    cos, sin = jnp.tile(cos, (B_SAMPLE, 1)), jnp.tile(sin, (B_SAMPLE, 1))
    intra, q_decay, k_decay, chunk_decay = _decay_tables(_log_gammas(), l)
    same_req = (jnp.arange(rows)[:, None] // l) == (jnp.arange(pad)[None, :] // l)
    dmat = jnp.where(same_req[None], jnp.pad(jnp.tile(intra, (1, B_SAMPLE, B_SAMPLE)),
                                              ((0, 0), (0, 0), (0, pad - rows))), 0.0)
    qdec = jnp.broadcast_to(jnp.tile(q_decay, (1, B_SAMPLE))[:, :, None], (N_HEADS, rows, HEAD_DIM))
    kdec = jnp.broadcast_to(jnp.tile(k_decay, (1, B_SAMPLE))[:, :, None], (N_HEADS, rows, HEAD_DIM))

    n_ffn_in = 7

    def body(cd_ref, *refs):
        cd = tuple(cd_ref[i] for i in range(N_HEADS))
        ffn_in, rest = refs[:n_ffn_in], refs[n_ffn_in:]
        mixer_in, (y_ref, merged_ref, sout_ref) = rest[:-3], rest[-3:]
        _ffn_kernel(*ffn_in, y_ref)
        _sample_mixer_kernel(cd, l, *mixer_in, merged_ref, sout_ref)

    row = _const_spec((1, D_MODEL))
    ptok = pl.BlockSpec((t_rows, D_MODEL), lambda i: (i, 0))
    state = pl.BlockSpec((B_SAMPLE, N_HEADS, HEAD_DIM, HEAD_DIM), lambda i: (i, 0, 0, 0))
    tok = pl.BlockSpec((rows, D_MODEL), lambda i: (i, 0))
    cache = pl.BlockSpec((B_SAMPLE, N_MEM * CACHE_ROWS, LANES), lambda i: (i, 0, 0))
    return pl.pallas_call(
        body,
        grid=(steps,),
        in_specs=[pl.BlockSpec(memory_space=pltpu.SMEM),
                  ptok, ptok, _const_spec((D_MODEL, D_MODEL)), row, _const_spec((D_MODEL, D_FF)),
                  _const_spec((D_FF, D_MODEL)), row,
                  pl.BlockSpec((rows, D_IN), lambda i: (i, 0)), tok,
                  _const_spec((rows, HALF)), _const_spec((rows, HALF)),
                  state, cache, cache, row,
                  _const_spec((N_HEADS, rows, pad)),
                  _const_spec((N_HEADS, rows, HEAD_DIM)), _const_spec((N_HEADS, rows, HEAD_DIM))],
        out_specs=[ptok, tok, state],
        out_shape=[jax.ShapeDtypeStruct((m, D_MODEL), F32),
                   jax.ShapeDtypeStruct((nb * l, D_MODEL), F32),
                   jax.ShapeDtypeStruct(state_ret.shape, F32)],
        compiler_params=_params(1),
        name="ffn_sample_mixer",
    )(chunk_decay, x, merged, w["w_out"], w["g_ffn"], w["w_up"], w["w_down"], w["g_final"],
      proj, conv, cos, sin, state_ret, cache_k, cache_v, w["ret_gn_g"], dmat, qdec, kdec)


def kernel(x_prompt, x_sample, mem_prompt, state_ret, state_conv, cache_mem_k, cache_mem_v, g_mix, w_in, ret_gn_g, conv_w, conv_b, conv_ln_g, conv_ln_b, w_conv_out, w_out, g_ffn, w_up, w_down, g_mem, w_mem_kv, g_final):
    assert state_ret.shape[0] == 1, "one layer"
    b_p, l_p, _ = x_prompt.shape
    b_s, l_s, _ = x_sample.shape
    w = {
        "g_mix": g_mix[0][None], "ret_gn_g": ret_gn_g[0][None],
        "conv_w": conv_w[0], "conv_b": conv_b[0][None], "conv_ln_g": conv_ln_g[0][None],
        "conv_ln_b": conv_ln_b[0][None], "w_conv_out": _column_slabs(w_conv_out[0].astype(BF16)),
        "g_ffn": g_ffn[0][None], "g_final": g_final[None],
    }

    xs = x_sample.reshape(b_s * l_s, D_MODEL)
    proj_s, w["w_in"] = _sample_proj(xs, w["g_mix"], w_in[0])
    conv_out_s, conv_s = _sample_conv(proj_s, jnp.transpose(state_conv[0], (1, 0, 2)), w, l_s)
    conv_s = jnp.transpose(conv_s, (1, 0, 2))

    mk, mv, mkb, mvb = _memkv(mem_prompt, g_mem[0][None], w_mem_kv[0])
    merged_p, ret_p, conv_p, w["w_out"], w["w_up"], w["w_down"] = _prompt_mixer(
        x_prompt, mkb, mvb, w, (w_out[0], w_up[0], w_down[0]))
    y_prompt, merged_s, ret_s = _ffn_with_sample_mixer(
        x_prompt.reshape(b_p * l_p, D_MODEL), merged_p.reshape(b_p * l_p, D_MODEL), w,
        proj_s, conv_out_s, state_ret[0],
        _cache_rows(cache_mem_k[0]), _cache_rows(cache_mem_v[0]), l_s)
    y_sample = _ffn(xs, merged_s, w)

    return (y_prompt.reshape(b_p, l_p, D_MODEL), y_sample.reshape(b_s, l_s, D_MODEL),
            ret_p[None], conv_p[None], mk[None], mv[None], ret_s[None], conv_s[None])
```

```python
import functools

import jax
import jax.numpy as jnp
from jax import lax
from jax.experimental import pallas as pl
from jax.experimental.pallas import tpu as pltpu

F32 = jnp.float32
BF16 = jnp.bfloat16

D_MODEL = 1024
N_HEADS = 4
HEAD_DIM = 256
HALF = HEAD_DIM // 2
D_FF = 4 * D_MODEL
CONV_WIDTH = 31
HIST = CONV_WIDTH - 1
N_MEM = 256
PAST_LEN = 16384
ROPE_BASE = 10000.0
EPS = 1e-6
QK_SCALE = HEAD_DIM ** -0.5

OFF_Q, OFF_K, OFF_V, OFF_G = 0, 1024, 2048, 3072
OFF_GLU_A, OFF_GLU_G, OFF_XA, OFF_GATE = 4096, 5120, 6144, 7168
D_IN = 10240

SUBLANES = 8
HIST_PAD = 32
T_PROMPT = 256
T_FFN = 512
FFN_ROWS = 256
B_SAMPLE = 2
VMEM_LIMIT = 56 * 1024 * 1024

LANES = 128
CACHE_ROWS = N_HEADS * HEAD_DIM // LANES

NT_DIMS = (((1,), (1,)), ((), ()))


def _cache_rows(c):
    b = c.shape[0]
    c = c.reshape(b, N_MEM, N_HEADS, HEAD_DIM // LANES, LANES).transpose(0, 1, 3, 2, 4)
    return c.reshape(b, N_MEM * CACHE_ROWS, LANES)


def _column_slabs(w):
    k, n = w.shape
    return w.reshape(k, n // HEAD_DIM, HEAD_DIM).transpose(1, 0, 2)


def _cache_from_rows(r):
    b = r.shape[0]
    r = r.reshape(b, N_MEM, HEAD_DIM // LANES, N_HEADS, LANES).transpose(0, 1, 3, 2, 4)
    return r.reshape(b, N_MEM, N_HEADS, HEAD_DIM)


def _dot(a, b):
    return jnp.dot(a, b, preferred_element_type=F32)


def _dot_nt(a, b):
    return lax.dot_general(a, b, NT_DIMS, preferred_element_type=F32)


def _rms(x, g):
    return x * lax.rsqrt(jnp.mean(x * x, axis=-1, keepdims=True) + EPS) * g


def _standardize(x):
    mu = jnp.mean(x, axis=-1, keepdims=True)
    xc = x - mu
    return xc * lax.rsqrt(jnp.mean(xc * xc, axis=-1, keepdims=True) + EPS)


def _silu(x):
    return x * jax.nn.sigmoid(x)


def _rope(x, cos, sin):
    x1, x2 = x[:, :HALF], x[:, HALF:]
    return jnp.concatenate([x1 * cos - x2 * sin, x2 * cos + x1 * sin], axis=-1)


def _const_spec(shape):
    return pl.BlockSpec(shape, lambda *_: (0,) * len(shape), pipeline_mode=pl.Buffered(1))


def _params(n_grid):
    return pltpu.CompilerParams(dimension_semantics=("arbitrary",) * n_grid,
                                vmem_limit_bytes=VMEM_LIMIT)


def _memkv_kernel(mem_ref, g_ref, w_ref, win_f_ref, k_ref, v_ref, kb_ref, vb_ref, win_b_ref,
                  wb_scr):
    @pl.when(pl.program_id(0) == 0)
    def _():
        wb_scr[...] = w_ref[...].astype(BF16)

    for q in range(win_b_ref.shape[0]):
        win_b_ref[q] = win_f_ref[:, q * HEAD_DIM:(q + 1) * HEAD_DIM].astype(BF16)

    h = _rms(mem_ref[0], g_ref[...]).astype(BF16)
    kv = _dot(h, wb_scr[...])
    k, v = kv[:, :D_MODEL], kv[:, D_MODEL:]
    for hd in range(N_HEADS):
        kb_ref[0, hd] = k[:, hd * HEAD_DIM:(hd + 1) * HEAD_DIM].astype(BF16)
        vb_ref[0, hd] = v[:, hd * HEAD_DIM:(hd + 1) * HEAD_DIM].astype(BF16)
    for hd in range(N_HEADS):
        for half in range(HEAD_DIM // LANES):
            rows = pl.ds(half * N_HEADS + hd, N_MEM, stride=CACHE_ROWS)
            cols = slice(hd * HEAD_DIM + half * LANES, hd * HEAD_DIM + (half + 1) * LANES)
            k_ref[0, rows, :] = k[:, cols]
            v_ref[0, rows, :] = v[:, cols]


def _memkv(mem, g_mem, w_mem_kv, w_in):
    b = mem.shape[0]
    n_slabs = D_IN // HEAD_DIM
    assert n_slabs % b == 0
    blk = pl.BlockSpec((1, N_MEM, D_MODEL), lambda i: (i, 0, 0))
    rows = pl.BlockSpec((1, N_MEM * CACHE_ROWS, LANES), lambda i: (i, 0, 0))
    slabs = pl.BlockSpec((1, N_HEADS, N_MEM, HEAD_DIM), lambda i: (i, 0, 0, 0))
    k, v, kb, vb, win_b = pl.pallas_call(
        _memkv_kernel,
        grid=(b,),
        in_specs=[blk, _const_spec((1, D_MODEL)), _const_spec((D_MODEL, 2 * D_MODEL)),
                  pl.BlockSpec((D_MODEL, D_IN // b), lambda i: (0, i))],
        out_specs=[rows, rows, slabs, slabs,
                   pl.BlockSpec((n_slabs // b, D_MODEL, HEAD_DIM), lambda i: (i, 0, 0))],
        out_shape=[jax.ShapeDtypeStruct((b, N_MEM * CACHE_ROWS, LANES), F32)] * 2
        + [jax.ShapeDtypeStruct((b, N_HEADS, N_MEM, HEAD_DIM), BF16)] * 2
        + [jax.ShapeDtypeStruct((n_slabs, D_MODEL, HEAD_DIM), BF16)],
        scratch_shapes=[pltpu.VMEM((D_MODEL, 2 * D_MODEL), BF16)],
        compiler_params=_params(1),
        name="memkv",
    )(mem, g_mem, w_mem_kv, w_in)
    return _cache_from_rows(k), _cache_from_rows(v), kb, vb, win_b


CONV_ROWS = 128


def _depthwise_conv(ext_ref, cw_ref, cb_ref, y_ref, t, lane_slices):
    first = HIST_PAD - HIST
    for j in lane_slices:
        cols = slice(j * LANES, (j + 1) * LANES)
        for r0 in range(0, t, CONV_ROWS):
            acc = jnp.broadcast_to(cb_ref[:, cols], (CONV_ROWS, LANES))
            for r in range(SUBLANES):
                rows = CONV_ROWS + (SUBLANES if r else 0)
                part = None
                for off in range(r, first + CONV_WIDTH, SUBLANES):
                    if off < first:
                        continue
                    w = off - first
                    base = r0 + off - r
                    term = ext_ref[base:base + rows, cols] * cw_ref[w:w + 1, cols]
                    part = term if part is None else part + term
                acc = acc + part[r:r + CONV_ROWS]
            y_ref[r0:r0 + CONV_ROWS, cols] = acc


def _prompt_mixer_kernel(chunk_decay,
                         x_ref, cos_ref, sin_ref, gmix_ref, win_ref, gn_ref, cw_ref, cb_ref,
                         lng_ref, lnb_ref, wco_ref, kb_ref, vb_ref, dmat_ref, qdec_ref, kdec_ref,
                         wout_f_ref, wup_f_ref, wdown_f_ref,
                         merged_ref, s_ref, cout_ref, wout_b_ref, wup_b_ref, wdown_b_ref,
                         ext_scr, y_scr, conv_scr, part_scr, g1_scr):
    t = T_PROMPT

    for f_ref, b_ref in ((wout_f_ref, wout_b_ref), (wup_f_ref, wup_b_ref),
                         (wdown_f_ref, wdown_b_ref)):
        b_ref[...] = f_ref[...].astype(BF16)

    @pl.when(pl.program_id(1) == 0)
    def _():
        s_ref[...] = jnp.zeros_like(s_ref)
        ext_scr[0:HIST_PAD, :] = jnp.zeros((HIST_PAD, D_MODEL), F32)

    h = _rms(x_ref[0], gmix_ref[...]).astype(BF16)

    def proj(off):
        return _dot(h, win_ref[off // HEAD_DIM])

    for j in range(D_MODEL // HEAD_DIM):
        cols = slice(j * HEAD_DIM, (j + 1) * HEAD_DIM)
        ext_scr[HIST_PAD:HIST_PAD + t, cols] = (
            proj(OFF_GLU_A + j * HEAD_DIM) * jax.nn.sigmoid(proj(OFF_GLU_G + j * HEAD_DIM)))

    cos, sin = cos_ref[...], sin_ref[...]
    conv_slices = D_MODEL // LANES // N_HEADS
    last = N_HEADS - 1
    for hd in range(N_HEADS):
        sl = slice(hd * HEAD_DIM, (hd + 1) * HEAD_DIM)
        _depthwise_conv(ext_scr, cw_ref, cb_ref, y_scr, t,
                        range(hd * conv_slices, (hd + 1) * conv_slices))
        qb = _rope(proj(OFF_Q + hd * HEAD_DIM), cos, sin).astype(BF16)
        kr = _rope(proj(OFF_K + hd * HEAD_DIM), cos, sin) * QK_SCALE
        vb = proj(OFF_V + hd * HEAD_DIM).astype(BF16)
        swish_g = _silu(proj(OFF_G + hd * HEAD_DIM))
        scores = _dot_nt(qb, kr.astype(BF16)) * dmat_ref[hd]
        qx = proj(OFF_XA + hd * HEAD_DIM).astype(BF16)
        s_old = s_ref[0, hd]
        o_cross = _dot(qb, s_old.astype(BF16)) * qdec_ref[hd]
        g0 = jax.nn.sigmoid(proj(OFF_GATE + hd * HEAD_DIM))
        o = _dot(scores.astype(BF16), vb) + o_cross
        s = _dot_nt(qx, kb_ref[0, hd]) * QK_SCALE
        g2 = jax.nn.sigmoid(proj(OFF_GATE + 2 * D_MODEL + hd * HEAD_DIM))
        kd_t = (kr * kdec_ref[hd]).T.astype(BF16)
        s_ref[0, hd] = s_old * chunk_decay[hd] + _dot(kd_t, vb)
        g1 = jax.nn.sigmoid(proj(OFF_GATE + D_MODEL + hd * HEAD_DIM))
        if hd == last:
            act = _silu(_standardize(y_scr[...]) * lng_ref[...] + lnb_ref[...])
            act = act.astype(BF16)
            for n in range(D_MODEL // HEAD_DIM):
                conv_scr[:, n * HEAD_DIM:(n + 1) * HEAD_DIM] = _dot(act, wco_ref[n])
            done = slice(0, hd * HEAD_DIM)
            merged_ref[0, :, done] = (
                part_scr[:, done] + g1_scr[:, done] * conv_scr[:, done]).astype(BF16)
        e = jnp.exp(s - jnp.max(s, axis=-1, keepdims=True))
        p = e / jnp.sum(e, axis=-1, keepdims=True)
        xa = _dot(p.astype(BF16), vb_ref[0, hd])
        ret = _standardize(o) * gn_ref[:, sl] * swish_g
        part = g0 * ret + g2 * xa
        if hd < last:
            part_scr[:, sl] = part
            g1_scr[:, sl] = g1
        else:
            merged_ref[0, :, sl] = (part + g1 * conv_scr[:, sl]).astype(BF16)

    cout_ref[0] = ext_scr[t + HIST_PAD - HIST:t + HIST_PAD, :]
    ext_scr[0:HIST_PAD, :] = ext_scr[t:t + HIST_PAD, :]


def _decay_tables(log_gamma, c):
    idx = jnp.arange(c, dtype=F32)
    diff = idx[:, None] - idx[None, :]
    causal = diff >= 0
    intra = jnp.where(causal[None],
                      jnp.exp(log_gamma[:, None, None] * jnp.where(causal, diff, 0.0)[None]), 0.0)
    q_decay = jnp.exp(log_gamma[:, None] * (idx + 1.0))
    k_decay = jnp.exp(log_gamma[:, None] * (c - 1.0 - idx))
    chunk_decay = jnp.exp(log_gamma * c)
    return intra, q_decay, k_decay, chunk_decay


def _log_gammas():
    return jnp.log1p(-jnp.exp2(-5.0 - jnp.arange(N_HEADS, dtype=F32)))


def _rope_tables(pos):
    inv = ROPE_BASE ** (-jnp.arange(0, HEAD_DIM, 2, dtype=F32) / HEAD_DIM)
    ang = pos.astype(F32)[:, None] * inv[None, :]
    return jnp.cos(ang), jnp.sin(ang)


def _prompt_mixer(x, kb, vb, w, mlp_weights):
    b, l, _ = x.shape
    t = T_PROMPT
    n_chunks = l // t
    steps = b * n_chunks

    def row_block(wt):
        assert wt.shape[0] % (steps * 2 * SUBLANES) == 0
        return pl.BlockSpec((wt.shape[0] // steps, wt.shape[1]), lambda i, j: (i * n_chunks + j, 0))

    mlp_blocks = [row_block(wt) for wt in mlp_weights]
    cos, sin = _rope_tables(jnp.arange(l))
    intra, q_decay, k_decay, chunk_decay = _decay_tables(_log_gammas(), t)
    qdec = jnp.broadcast_to(q_decay[:, :, None], (N_HEADS, t, HEAD_DIM))
    kdec = jnp.broadcast_to(k_decay[:, :, None], (N_HEADS, t, HEAD_DIM))

    def body(cd_ref, *refs):
        cd = tuple(cd_ref[i] for i in range(N_HEADS))
        _prompt_mixer_kernel(cd, *refs)

    tok = pl.BlockSpec((1, t, D_MODEL), lambda i, j: (i, j, 0))
    rope = pl.BlockSpec((t, HALF), lambda i, j: (j, 0))
    mem = pl.BlockSpec((1, N_HEADS, N_MEM, HEAD_DIM), lambda i, j: (i, 0, 0, 0))
    row = _const_spec((1, D_MODEL))
    table = _const_spec((N_HEADS, t, HEAD_DIM))
    return pl.pallas_call(
        body,
        grid=(b, n_chunks),
        in_specs=[pl.BlockSpec(memory_space=pltpu.SMEM),
                  tok, rope, rope, row, _const_spec((D_IN // HEAD_DIM, D_MODEL, HEAD_DIM)), row,
                  _const_spec((CONV_WIDTH, D_MODEL)), row, row, row,
                  _const_spec((D_MODEL // HEAD_DIM, D_MODEL, HEAD_DIM)), mem, mem,
                  table, table, table] + mlp_blocks,
        out_specs=[tok,
                   pl.BlockSpec((1, N_HEADS, HEAD_DIM, HEAD_DIM), lambda i, j: (i, 0, 0, 0)),
                   pl.BlockSpec((1, HIST, D_MODEL), lambda i, j: (i, 0, 0))] + mlp_blocks,
        out_shape=[jax.ShapeDtypeStruct((b, l, D_MODEL), BF16),
                   jax.ShapeDtypeStruct((b, N_HEADS, HEAD_DIM, HEAD_DIM), F32),
                   jax.ShapeDtypeStruct((b, HIST, D_MODEL), F32)]
        + [jax.ShapeDtypeStruct(wt.shape, BF16) for wt in mlp_weights],
        scratch_shapes=[pltpu.VMEM((HIST_PAD + t, D_MODEL), F32)]
        + [pltpu.VMEM((t, D_MODEL), F32)] * 4,
        compiler_params=_params(2),
        name="prompt_mixer",
    )(chunk_decay, x, cos, sin, w["g_mix"], w["w_in"], w["ret_gn_g"], w["conv_w"],
      w["conv_b"], w["conv_ln_g"], w["conv_ln_b"], w["w_conv_out"], kb, vb, intra, qdec, kdec,
      *mlp_weights)


def _ffn_kernel(x_ref, m_ref, wout_ref, gffn_ref, wup_ref, wdown_ref, gfin_ref, y_ref):
    groups = [slice(r, r + FFN_ROWS) for r in range(0, x_ref.shape[0], FFN_ROWS)]
    x1 = [x_ref[g, :] + _dot(m_ref[g, :].astype(BF16), wout_ref[...]) for g in groups]
    h2 = [_rms(v, gffn_ref[...]).astype(BF16) for v in x1]
    a = [jnp.square(jnp.maximum(_dot(v, wup_ref[...]), 0.0)).astype(BF16) for v in h2]
    for g, v, act in zip(groups, x1, a):
        y_ref[g, :] = _rms(v + _dot(act, wdown_ref[...]), gfin_ref[...])


def _ffn(x, merged, w):
    m = x.shape[0]
    tok = pl.BlockSpec((T_FFN, D_MODEL), lambda i: (i, 0))
    row = _const_spec((1, D_MODEL))
    return pl.pallas_call(
        _ffn_kernel,
        grid=(m // T_FFN,),
        in_specs=[tok, tok, _const_spec((D_MODEL, D_MODEL)), row, _const_spec((D_MODEL, D_FF)),
                  _const_spec((D_FF, D_MODEL)), row],
        out_specs=tok,
        out_shape=jax.ShapeDtypeStruct((m, D_MODEL), F32),
        compiler_params=_params(1),
        name="ffn",
    )(x, merged, w["w_out"], w["g_ffn"], w["w_up"], w["w_down"], w["g_final"])


SAMPLE_PROJ_COLS = 1024


def _sample_proj_kernel(x_ref, g_ref, w_ref, o_ref):
    h = _rms(x_ref[...], g_ref[...]).astype(BF16)
    for q in range(SAMPLE_PROJ_COLS // HEAD_DIM):
        o_ref[:, q * HEAD_DIM:(q + 1) * HEAD_DIM] = _dot(h, w_ref[q])


def _sample_proj(x, g_mix, w_in_slabs):
    m = x.shape[0]
    slabs = SAMPLE_PROJ_COLS // HEAD_DIM
    return pl.pallas_call(
        _sample_proj_kernel,
        grid=(D_IN // SAMPLE_PROJ_COLS,),
        in_specs=[_const_spec((m, D_MODEL)), _const_spec((1, D_MODEL)),
                  pl.BlockSpec((slabs, D_MODEL, HEAD_DIM), lambda j: (j, 0, 0))],
        out_specs=pl.BlockSpec((m, SAMPLE_PROJ_COLS), lambda j: (0, j)),
        out_shape=jax.ShapeDtypeStruct((m, D_IN), F32),
        compiler_params=_params(1),
        name="sample_proj",
    )(x, g_mix, w_in_slabs)


def _head_rows(ref, bi, hd):
    halves = [ref[bi, pl.ds(half * N_HEADS + hd, N_MEM, stride=CACHE_ROWS), :]
              for half in range(HEAD_DIM // LANES)]
    return jnp.concatenate(halves, axis=-1)


SAMPLE_CONV_LANES = LANES


def _sample_conv_kernel(l, a_ref, g_ref, st_ref, cw_ref, cb_ref, lng_ref, lnb_ref, wco_ref,
                        conv_ref, st_out_ref, u_scr, y_scr):
    j = pl.program_id(0)
    nb = st_ref.shape[1]
    u_scr[...] = a_ref[...] * jax.nn.sigmoid(g_ref[...])

    def plane(k):
        return st_ref[k] if k < HIST else u_scr[pl.ds(k - HIST, nb, stride=l), :]

    for w in range(HIST):
        st_out_ref[w] = plane(w + l)
    for i in range(l):
        acc = jnp.broadcast_to(cb_ref[...], (nb, SAMPLE_CONV_LANES))
        for w in range(CONV_WIDTH):
            acc = acc + plane(i + w) * cw_ref[w:w + 1, :]
        y_scr[j, pl.ds(i, nb, stride=l), :] = acc

    @pl.when(j == pl.num_programs(0) - 1)
    def _():
        y = jnp.concatenate([y_scr[n] for n in range(D_MODEL // SAMPLE_CONV_LANES)], axis=-1)
        act = _silu(_standardize(y) * lng_ref[...] + lnb_ref[...]).astype(BF16)
        for n in range(D_MODEL // HEAD_DIM):
            conv_ref[:, n * HEAD_DIM:(n + 1) * HEAD_DIM] = _dot(act, wco_ref[n])


def _sample_conv(proj, state_t, w, l):
    m = proj.shape[0]
    nb = state_t.shape[1]
    lanes = SAMPLE_CONV_LANES
    glu_a = pl.BlockSpec((m, lanes), lambda j: (0, OFF_GLU_A // lanes + j))
    glu_g = pl.BlockSpec((m, lanes), lambda j: (0, OFF_GLU_G // lanes + j))
    st = pl.BlockSpec((HIST, nb, lanes), lambda j: (0, 0, j))
    row_blk = pl.BlockSpec((1, lanes), lambda j: (0, j))
    return pl.pallas_call(
        functools.partial(_sample_conv_kernel, l),
        grid=(D_MODEL // lanes,),
        in_specs=[glu_a, glu_g, st, pl.BlockSpec((CONV_WIDTH, lanes), lambda j: (0, j)), row_blk,
                  _const_spec((1, D_MODEL)), _const_spec((1, D_MODEL)),
                  _const_spec((D_MODEL // HEAD_DIM, D_MODEL, HEAD_DIM))],
        out_specs=[pl.BlockSpec((m, D_MODEL), lambda j: (0, 0)), st],
        out_shape=[jax.ShapeDtypeStruct((m, D_MODEL), F32),
                   jax.ShapeDtypeStruct(state_t.shape, F32)],
        scratch_shapes=[pltpu.VMEM((m, lanes), F32),
                        pltpu.VMEM((D_MODEL // lanes, m, lanes), F32)],
        compiler_params=_params(1),
        name="sample_conv",
    )(proj, proj, state_t, w["conv_w"], w["conv_b"], w["conv_ln_g"], w["conv_ln_b"],
      w["w_conv_out"])


def _sample_mixer_kernel(chunk_decay, l,
                         p_ref, conv_ref, cos_ref, sin_ref, s_ref, k_ref, v_ref,
                         gn_ref, dmat_ref, qdec_ref, kdec_ref,
                         merged_ref, sout_ref):
    rows = B_SAMPLE * l
    pad = 128
    row_id = lax.broadcasted_iota(jnp.int32, (rows, HEAD_DIM), 0)
    col_id = lax.broadcasted_iota(jnp.int32, (HEAD_DIM, pad), 1)
    row_of = [(row_id >= bi * l) & (row_id < (bi + 1) * l) for bi in range(B_SAMPLE)]
    col_of = [(col_id >= bi * l) & (col_id < (bi + 1) * l) for bi in range(B_SAMPLE)]

    def proj(off, n=HEAD_DIM):
        return p_ref[:, off:off + n]

    cos, sin = cos_ref[...], sin_ref[...]
    zeros_pad = jnp.zeros((pad - rows, HEAD_DIM), F32)

    def first_stage(hd):
        qb = _rope(proj(OFF_Q + hd * HEAD_DIM), cos, sin).astype(BF16)
        kr = _rope(proj(OFF_K + hd * HEAD_DIM), cos, sin) * QK_SCALE
        v = proj(OFF_V + hd * HEAD_DIM)
        kb_pad = jnp.concatenate([kr, zeros_pad], axis=0).astype(BF16)
        vb_pad = jnp.concatenate([v, zeros_pad], axis=0).astype(BF16)
        scores = _dot_nt(qb, kb_pad) * dmat_ref[hd]
        kd_t = jnp.concatenate([kr * kdec_ref[hd], zeros_pad], axis=0).T
        qx = proj(OFF_XA + hd * HEAD_DIM).astype(BF16)
        o_cross = jnp.zeros((rows, HEAD_DIM), F32)
        s = jnp.zeros((rows, N_MEM), F32)
        for bi in range(B_SAMPLE):
            s_old = s_ref[bi, hd]
            o_cross = jnp.where(row_of[bi], _dot(qb, s_old.astype(BF16)), o_cross)
            kd_bi = jnp.where(col_of[bi], kd_t, 0.0).astype(BF16)
            sout_ref[bi, hd] = s_old * chunk_decay[hd] + _dot(kd_bi, vb_pad)
            s = jnp.where(row_of[bi], _dot_nt(qx, _head_rows(k_ref, bi, hd).astype(BF16)), s)
        return scores, vb_pad, o_cross, s * QK_SCALE

    def second_stage(hd, scores, vb_pad, o_cross, s):
        sl = slice(hd * HEAD_DIM, (hd + 1) * HEAD_DIM)
        o = _dot(scores.astype(BF16), vb_pad) + o_cross * qdec_ref[hd]
        ret = _standardize(o) * gn_ref[:, sl] * _silu(proj(OFF_G + hd * HEAD_DIM))
        e = jnp.exp(s - jnp.max(s, axis=-1, keepdims=True))
        pb = (e / jnp.sum(e, axis=-1, keepdims=True)).astype(BF16)
        xa = jnp.zeros((rows, HEAD_DIM), F32)
        for bi in range(B_SAMPLE):
            xa = jnp.where(row_of[bi], _dot(pb, _head_rows(v_ref, bi, hd).astype(BF16)), xa)
        g0 = jax.nn.sigmoid(proj(OFF_GATE + hd * HEAD_DIM))
        g2 = jax.nn.sigmoid(proj(OFF_GATE + 2 * D_MODEL + hd * HEAD_DIM))
        return g0 * ret + g2 * xa

    parts = []
    staged = first_stage(0)
    for hd in range(N_HEADS):
        following = first_stage(hd + 1) if hd + 1 < N_HEADS else None
        parts.append(second_stage(hd, *staged))
        staged = following
    g1 = jax.nn.sigmoid(proj(OFF_GATE + D_MODEL, D_MODEL))
    merged_ref[...] = jnp.concatenate(parts, axis=-1) + g1 * conv_ref[...]


def _ffn_with_sample_mixer(x, merged, w, proj, conv, state_ret, cache_k, cache_v, l):
    m = x.shape[0]
    nb = state_ret.shape[0]
    steps = nb // B_SAMPLE
    t_rows = m // steps
    assert steps * B_SAMPLE == nb and steps * t_rows == m and t_rows % FFN_ROWS == 0
    rows = B_SAMPLE * l
    pad = 128
    cos, sin = _rope_tables(PAST_LEN + jnp.arange(l))
    cos, sin = jnp.tile(cos, (B_SAMPLE, 1)), jnp.tile(sin, (B_SAMPLE, 1))
    intra, q_decay, k_decay, chunk_decay = _decay_tables(_log_gammas(), l)
    same_req = (jnp.arange(rows)[:, None] // l) == (jnp.arange(pad)[None, :] // l)
    dmat = jnp.where(same_req[None], jnp.pad(jnp.tile(intra, (1, B_SAMPLE, B_SAMPLE)),
                                              ((0, 0), (0, 0), (0, pad - rows))), 0.0)
    qdec = jnp.broadcast_to(jnp.tile(q_decay, (1, B_SAMPLE))[:, :, None], (N_HEADS, rows, HEAD_DIM))
    kdec = jnp.broadcast_to(jnp.tile(k_decay, (1, B_SAMPLE))[:, :, None], (N_HEADS, rows, HEAD_DIM))

    n_ffn_in = 7

    def body(cd_ref, *refs):
        cd = tuple(cd_ref[i] for i in range(N_HEADS))
        ffn_in, rest = refs[:n_ffn_in], refs[n_ffn_in:]
        mixer_in, (y_ref, merged_ref, sout_ref) = rest[:-3], rest[-3:]
        _ffn_kernel(*ffn_in, y_ref)
        _sample_mixer_kernel(cd, l, *mixer_in, merged_ref, sout_ref)

    row = _const_spec((1, D_MODEL))
    ptok = pl.BlockSpec((t_rows, D_MODEL), lambda i: (i, 0))
    state = pl.BlockSpec((B_SAMPLE, N_HEADS, HEAD_DIM, HEAD_DIM), lambda i: (i, 0, 0, 0))
    tok = pl.BlockSpec((rows, D_MODEL), lambda i: (i, 0))
    cache = pl.BlockSpec((B_SAMPLE, N_MEM * CACHE_ROWS, LANES), lambda i: (i, 0, 0))
    return pl.pallas_call(
        body,
        grid=(steps,),
        in_specs=[pl.BlockSpec(memory_space=pltpu.SMEM),
                  ptok, ptok, _const_spec((D_MODEL, D_MODEL)), row, _const_spec((D_MODEL, D_FF)),
                  _const_spec((D_FF, D_MODEL)), row,
                  pl.BlockSpec((rows, D_IN), lambda i: (i, 0)), tok,
                  _const_spec((rows, HALF)), _const_spec((rows, HALF)),
                  state, cache, cache, row,
                  _const_spec((N_HEADS, rows, pad)),
                  _const_spec((N_HEADS, rows, HEAD_DIM)), _const_spec((N_HEADS, rows, HEAD_DIM))],
        out_specs=[ptok, tok, state],
        out_shape=[jax.ShapeDtypeStruct((m, D_MODEL), F32),
                   jax.ShapeDtypeStruct((nb * l, D_MODEL), F32),
                   jax.ShapeDtypeStruct(state_ret.shape, F32)],
        compiler_params=_params(1),
        name="ffn_sample_mixer",
    )(chunk_decay, x, merged, w["w_out"], w["g_ffn"], w["w_up"], w["w_down"], w["g_final"],
      proj, conv, cos, sin, state_ret, cache_k, cache_v, w["ret_gn_g"], dmat, qdec, kdec)


def kernel(x_prompt, x_sample, mem_prompt, state_ret, state_conv, cache_mem_k, cache_mem_v, g_mix, w_in, ret_gn_g, conv_w, conv_b, conv_ln_g, conv_ln_b, w_conv_out, w_out, g_ffn, w_up, w_down, g_mem, w_mem_kv, g_final):
    assert state_ret.shape[0] == 1, "one layer"
    b_p, l_p, _ = x_prompt.shape
    b_s, l_s, _ = x_sample.shape
    w = {
        "g_mix": g_mix[0][None], "ret_gn_g": ret_gn_g[0][None],
        "conv_w": conv_w[0], "conv_b": conv_b[0][None], "conv_ln_g": conv_ln_g[0][None],
        "conv_ln_b": conv_ln_b[0][None], "w_conv_out": _column_slabs(w_conv_out[0].astype(BF16)),
        "g_ffn": g_ffn[0][None], "g_final": g_final[None],
    }

    mk, mv, mkb, mvb, w["w_in"] = _memkv(mem_prompt, g_mem[0][None], w_mem_kv[0], w_in[0])

    xs = x_sample.reshape(b_s * l_s, D_MODEL)
    proj_s = _sample_proj(xs, w["g_mix"], w["w_in"])
    conv_out_s, conv_s = _sample_conv(proj_s, jnp.transpose(state_conv[0], (1, 0, 2)), w, l_s)
    conv_s = jnp.transpose(conv_s, (1, 0, 2))

    merged_p, ret_p, conv_p, w["w_out"], w["w_up"], w["w_down"] = _prompt_mixer(
        x_prompt, mkb, mvb, w, (w_out[0], w_up[0], w_down[0]))
    y_prompt, merged_s, ret_s = _ffn_with_sample_mixer(
        x_prompt.reshape(b_p * l_p, D_MODEL), merged_p.reshape(b_p * l_p, D_MODEL), w,
        proj_s, conv_out_s, state_ret[0],
        _cache_rows(cache_mem_k[0]), _cache_rows(cache_mem_v[0]), l_s)
    y_sample = _ffn(xs, merged_s, w)

    return (y_prompt.reshape(b_p, l_p, D_MODEL), y_sample.reshape(b_s, l_s, D_MODEL),
            ret_p[None], conv_p[None], mk[None], mv[None], ret_s[None], conv_s[None])
```

```python
import functools

import jax
import jax.numpy as jnp
from jax import lax
from jax.experimental import pallas as pl
from jax.experimental.pallas import tpu as pltpu

F32 = jnp.float32
BF16 = jnp.bfloat16

D_MODEL = 1024
N_HEADS = 4
HEAD_DIM = 256
HALF = HEAD_DIM // 2
D_FF = 4 * D_MODEL
CONV_WIDTH = 31
HIST = CONV_WIDTH - 1
N_MEM = 256
PAST_LEN = 16384
ROPE_BASE = 10000.0
EPS = 1e-6
QK_SCALE = HEAD_DIM ** -0.5

OFF_Q, OFF_K, OFF_V, OFF_G = 0, 1024, 2048, 3072
OFF_GLU_A, OFF_GLU_G, OFF_XA, OFF_GATE = 4096, 5120, 6144, 7168
D_IN = 10240

SUBLANES = 8
HIST_PAD = 32
T_PROMPT = 256
T_FFN = 512
FFN_ROWS = 256
B_SAMPLE = 2
VMEM_LIMIT = 56 * 1024 * 1024

LANES = 128
CACHE_ROWS = N_HEADS * HEAD_DIM // LANES

NT_DIMS = (((1,), (1,)), ((), ()))


def _cache_rows(c):
    b = c.shape[0]
    c = c.reshape(b, N_MEM, N_HEADS, HEAD_DIM // LANES, LANES).transpose(0, 1, 3, 2, 4)
    return c.reshape(b, N_MEM * CACHE_ROWS, LANES)


def _column_slabs(w):
    k, n = w.shape
    return w.reshape(k, n // HEAD_DIM, HEAD_DIM).transpose(1, 0, 2)


def _cache_from_rows(r):
    b = r.shape[0]
    r = r.reshape(b, N_MEM, HEAD_DIM // LANES, N_HEADS, LANES).transpose(0, 1, 3, 2, 4)
    return r.reshape(b, N_MEM, N_HEADS, HEAD_DIM)


def _dot(a, b):
    return jnp.dot(a, b, preferred_element_type=F32)


def _dot_nt(a, b):
    return lax.dot_general(a, b, NT_DIMS, preferred_element_type=F32)


def _rms(x, g):
    return x * lax.rsqrt(jnp.mean(x * x, axis=-1, keepdims=True) + EPS) * g


def _standardize(x):
    mu = jnp.mean(x, axis=-1, keepdims=True)
    xc = x - mu
    return xc * lax.rsqrt(jnp.mean(xc * xc, axis=-1, keepdims=True) + EPS)


def _silu(x):
    return x * jax.nn.sigmoid(x)


def _rope(x, cos, sin):
    x1, x2 = x[:, :HALF], x[:, HALF:]
    return jnp.concatenate([x1 * cos - x2 * sin, x2 * cos + x1 * sin], axis=-1)


def _const_spec(shape):
    return pl.BlockSpec(shape, lambda *_: (0,) * len(shape), pipeline_mode=pl.Buffered(1))


def _params(n_grid):
    return pltpu.CompilerParams(dimension_semantics=("arbitrary",) * n_grid,
                                vmem_limit_bytes=VMEM_LIMIT)


def _memkv_kernel(mem_ref, g_ref, w_ref, k_ref, v_ref, kb_ref, vb_ref, wb_scr):
    @pl.when(pl.program_id(0) == 0)
    def _():
        wb_scr[...] = w_ref[...].astype(BF16)

    h = _rms(mem_ref[0], g_ref[...]).astype(BF16)
    kv = _dot(h, wb_scr[...])
    k, v = kv[:, :D_MODEL], kv[:, D_MODEL:]
    for hd in range(N_HEADS):
        kb_ref[0, hd] = k[:, hd * HEAD_DIM:(hd + 1) * HEAD_DIM].astype(BF16)
        vb_ref[0, hd] = v[:, hd * HEAD_DIM:(hd + 1) * HEAD_DIM].astype(BF16)
    for hd in range(N_HEADS):
        for half in range(HEAD_DIM // LANES):
            rows = pl.ds(half * N_HEADS + hd, N_MEM, stride=CACHE_ROWS)
            cols = slice(hd * HEAD_DIM + half * LANES, hd * HEAD_DIM + (half + 1) * LANES)
            k_ref[0, rows, :] = k[:, cols]
            v_ref[0, rows, :] = v[:, cols]


def _memkv(mem, g_mem, w_mem_kv):
    b = mem.shape[0]
    blk = pl.BlockSpec((1, N_MEM, D_MODEL), lambda i: (i, 0, 0))
    rows = pl.BlockSpec((1, N_MEM * CACHE_ROWS, LANES), lambda i: (i, 0, 0))
    slabs = pl.BlockSpec((1, N_HEADS, N_MEM, HEAD_DIM), lambda i: (i, 0, 0, 0))
    k, v, kb, vb = pl.pallas_call(
        _memkv_kernel,
        grid=(b,),
        in_specs=[blk, _const_spec((1, D_MODEL)), _const_spec((D_MODEL, 2 * D_MODEL))],
        out_specs=[rows, rows, slabs, slabs],
        out_shape=[jax.ShapeDtypeStruct((b, N_MEM * CACHE_ROWS, LANES), F32)] * 2
        + [jax.ShapeDtypeStruct((b, N_HEADS, N_MEM, HEAD_DIM), BF16)] * 2,
        scratch_shapes=[pltpu.VMEM((D_MODEL, 2 * D_MODEL), BF16)],
        compiler_params=_params(1),
        name="memkv",
    )(mem, g_mem, w_mem_kv)
    return _cache_from_rows(k), _cache_from_rows(v), kb, vb


CONV_ROWS = 128


def _depthwise_conv(ext_ref, cw_ref, cb_ref, y_ref, t, lane_slices):
    first = HIST_PAD - HIST
    for j in lane_slices:
        cols = slice(j * LANES, (j + 1) * LANES)
        for r0 in range(0, t, CONV_ROWS):
            acc = jnp.broadcast_to(cb_ref[:, cols], (CONV_ROWS, LANES))
            for r in range(SUBLANES):
                rows = CONV_ROWS + (SUBLANES if r else 0)
                part = None
                for off in range(r, first + CONV_WIDTH, SUBLANES):
                    if off < first:
                        continue
                    w = off - first
                    base = r0 + off - r
                    term = ext_ref[base:base + rows, cols] * cw_ref[w:w + 1, cols]
                    part = term if part is None else part + term
                acc = acc + part[r:r + CONV_ROWS]
            y_ref[r0:r0 + CONV_ROWS, cols] = acc


def _prompt_mixer_kernel(chunk_decay,
                         x_ref, cos_ref, sin_ref, gmix_ref, win_ref, gn_ref, cw_ref, cb_ref,
                         lng_ref, lnb_ref, wco_ref, kb_ref, vb_ref, dmat_ref, qdec_ref, kdec_ref,
                         wout_f_ref, wup_f_ref, wdown_f_ref,
                         merged_ref, s_ref, cout_ref, wout_b_ref, wup_b_ref, wdown_b_ref,
                         ext_scr, y_scr, conv_scr, part_scr, g1_scr):
    t = T_PROMPT

    for f_ref, b_ref in ((wout_f_ref, wout_b_ref), (wup_f_ref, wup_b_ref),
                         (wdown_f_ref, wdown_b_ref)):
        b_ref[...] = f_ref[...].astype(BF16)

    @pl.when(pl.program_id(1) == 0)
    def _():
        s_ref[...] = jnp.zeros_like(s_ref)
        ext_scr[0:HIST_PAD, :] = jnp.zeros((HIST_PAD, D_MODEL), F32)

    h = _rms(x_ref[0], gmix_ref[...]).astype(BF16)

    def proj(off):
        return _dot(h, win_ref[off // HEAD_DIM])

    for j in range(D_MODEL // HEAD_DIM):
        cols = slice(j * HEAD_DIM, (j + 1) * HEAD_DIM)
        ext_scr[HIST_PAD:HIST_PAD + t, cols] = (
            proj(OFF_GLU_A + j * HEAD_DIM) * jax.nn.sigmoid(proj(OFF_GLU_G + j * HEAD_DIM)))

    cos, sin = cos_ref[...], sin_ref[...]
    conv_slices = D_MODEL // LANES // N_HEADS
    last = N_HEADS - 1
    for hd in range(N_HEADS):
        sl = slice(hd * HEAD_DIM, (hd + 1) * HEAD_DIM)
        _depthwise_conv(ext_scr, cw_ref, cb_ref, y_scr, t,
                        range(hd * conv_slices, (hd + 1) * conv_slices))
        qb = _rope(proj(OFF_Q + hd * HEAD_DIM), cos, sin).astype(BF16)
        kr = _rope(proj(OFF_K + hd * HEAD_DIM), cos, sin) * QK_SCALE
        vb = proj(OFF_V + hd * HEAD_DIM).astype(BF16)
        swish_g = _silu(proj(OFF_G + hd * HEAD_DIM))
        scores = _dot_nt(qb, kr.astype(BF16)) * dmat_ref[hd]
        qx = proj(OFF_XA + hd * HEAD_DIM).astype(BF16)
        s_old = s_ref[0, hd]
        o_cross = _dot(qb, s_old.astype(BF16)) * qdec_ref[hd]
        g0 = jax.nn.sigmoid(proj(OFF_GATE + hd * HEAD_DIM))
        o = _dot(scores.astype(BF16), vb) + o_cross
        s = _dot_nt(qx, kb_ref[0, hd]) * QK_SCALE
        g2 = jax.nn.sigmoid(proj(OFF_GATE + 2 * D_MODEL + hd * HEAD_DIM))
        kd_t = (kr * kdec_ref[hd]).T.astype(BF16)
        s_ref[0, hd] = s_old * chunk_decay[hd] + _dot(kd_t, vb)
        g1 = jax.nn.sigmoid(proj(OFF_GATE + D_MODEL + hd * HEAD_DIM))
        if hd == last:
            act = _silu(_standardize(y_scr[...]) * lng_ref[...] + lnb_ref[...])
            act = act.astype(BF16)
            for n in range(D_MODEL // HEAD_DIM):
                conv_scr[:, n * HEAD_DIM:(n + 1) * HEAD_DIM] = _dot(act, wco_ref[n])
            done = slice(0, hd * HEAD_DIM)
            merged_ref[0, :, done] = (
                part_scr[:, done] + g1_scr[:, done] * conv_scr[:, done]).astype(BF16)
        e = jnp.exp(s - jnp.max(s, axis=-1, keepdims=True))
        p = e / jnp.sum(e, axis=-1, keepdims=True)
        xa = _dot(p.astype(BF16), vb_ref[0, hd])
        ret = _standardize(o) * gn_ref[:, sl] * swish_g
        part = g0 * ret + g2 * xa
        if hd < last:
            part_scr[:, sl] = part
            g1_scr[:, sl] = g1
        else:
            merged_ref[0, :, sl] = (part + g1 * conv_scr[:, sl]).astype(BF16)

    cout_ref[0] = ext_scr[t + HIST_PAD - HIST:t + HIST_PAD, :]
    ext_scr[0:HIST_PAD, :] = ext_scr[t:t + HIST_PAD, :]


def _decay_tables(log_gamma, c):
    idx = jnp.arange(c, dtype=F32)
    diff = idx[:, None] - idx[None, :]
    causal = diff >= 0
    intra = jnp.where(causal[None],
                      jnp.exp(log_gamma[:, None, None] * jnp.where(causal, diff, 0.0)[None]), 0.0)
    q_decay = jnp.exp(log_gamma[:, None] * (idx + 1.0))
    k_decay = jnp.exp(log_gamma[:, None] * (c - 1.0 - idx))
    chunk_decay = jnp.exp(log_gamma * c)
    return intra, q_decay, k_decay, chunk_decay


def _log_gammas():
    return jnp.log1p(-jnp.exp2(-5.0 - jnp.arange(N_HEADS, dtype=F32)))


def _rope_tables(pos):
    inv = ROPE_BASE ** (-jnp.arange(0, HEAD_DIM, 2, dtype=F32) / HEAD_DIM)
    ang = pos.astype(F32)[:, None] * inv[None, :]
    return jnp.cos(ang), jnp.sin(ang)


def _prompt_mixer(x, kb, vb, w, mlp_weights):
    b, l, _ = x.shape
    t = T_PROMPT
    n_chunks = l // t
    steps = b * n_chunks

    def row_block(wt):
        assert wt.shape[0] % (steps * 2 * SUBLANES) == 0
        return pl.BlockSpec((wt.shape[0] // steps, wt.shape[1]), lambda i, j: (i * n_chunks + j, 0))

    mlp_blocks = [row_block(wt) for wt in mlp_weights]
    cos, sin = _rope_tables(jnp.arange(l))
    intra, q_decay, k_decay, chunk_decay = _decay_tables(_log_gammas(), t)
    qdec = jnp.broadcast_to(q_decay[:, :, None], (N_HEADS, t, HEAD_DIM))
    kdec = jnp.broadcast_to(k_decay[:, :, None], (N_HEADS, t, HEAD_DIM))

    def body(cd_ref, *refs):
        cd = tuple(cd_ref[i] for i in range(N_HEADS))
        _prompt_mixer_kernel(cd, *refs)

    tok = pl.BlockSpec((1, t, D_MODEL), lambda i, j: (i, j, 0))
    rope = pl.BlockSpec((t, HALF), lambda i, j: (j, 0))
    mem = pl.BlockSpec((1, N_HEADS, N_MEM, HEAD_DIM), lambda i, j: (i, 0, 0, 0))
    row = _const_spec((1, D_MODEL))
    table = _const_spec((N_HEADS, t, HEAD_DIM))
    return pl.pallas_call(
        body,
        grid=(b, n_chunks),
        in_specs=[pl.BlockSpec(memory_space=pltpu.SMEM),
                  tok, rope, rope, row, _const_spec((D_IN // HEAD_DIM, D_MODEL, HEAD_DIM)), row,
                  _const_spec((CONV_WIDTH, D_MODEL)), row, row, row,
                  _const_spec((D_MODEL // HEAD_DIM, D_MODEL, HEAD_DIM)), mem, mem,
                  table, table, table] + mlp_blocks,
        out_specs=[tok,
                   pl.BlockSpec((1, N_HEADS, HEAD_DIM, HEAD_DIM), lambda i, j: (i, 0, 0, 0)),
                   pl.BlockSpec((1, HIST, D_MODEL), lambda i, j: (i, 0, 0))] + mlp_blocks,
        out_shape=[jax.ShapeDtypeStruct((b, l, D_MODEL), BF16),
                   jax.ShapeDtypeStruct((b, N_HEADS, HEAD_DIM, HEAD_DIM), F32),
                   jax.ShapeDtypeStruct((b, HIST, D_MODEL), F32)]
        + [jax.ShapeDtypeStruct(wt.shape, BF16) for wt in mlp_weights],
        scratch_shapes=[pltpu.VMEM((HIST_PAD + t, D_MODEL), F32)]
        + [pltpu.VMEM((t, D_MODEL), F32)] * 4,
        compiler_params=_params(2),
        name="prompt_mixer",
    )(chunk_decay, x, cos, sin, w["g_mix"], w["w_in"], w["ret_gn_g"], w["conv_w"],
      w["conv_b"], w["conv_ln_g"], w["conv_ln_b"], w["w_conv_out"], kb, vb, intra, qdec, kdec,
      *mlp_weights)


def _ffn_kernel(x_ref, m_ref, wout_ref, gffn_ref, wup_ref, wdown_ref, gfin_ref, y_ref):
    groups = [slice(r, r + FFN_ROWS) for r in range(0, x_ref.shape[0], FFN_ROWS)]
    x1 = [x_ref[g, :] + _dot(m_ref[g, :].astype(BF16), wout_ref[...]) for g in groups]
    h2 = [_rms(v, gffn_ref[...]).astype(BF16) for v in x1]
    a = [jnp.square(jnp.maximum(_dot(v, wup_ref[...]), 0.0)).astype(BF16) for v in h2]
    for g, v, act in zip(groups, x1, a):
        y_ref[g, :] = _rms(v + _dot(act, wdown_ref[...]), gfin_ref[...])


def _ffn(x, merged, w):
    m = x.shape[0]
    tok = pl.BlockSpec((T_FFN, D_MODEL), lambda i: (i, 0))
    row = _const_spec((1, D_MODEL))
    return pl.pallas_call(
        _ffn_kernel,
        grid=(m // T_FFN,),
        in_specs=[tok, tok, _const_spec((D_MODEL, D_MODEL)), row, _const_spec((D_MODEL, D_FF)),
                  _const_spec((D_FF, D_MODEL)), row],
        out_specs=tok,
        out_shape=jax.ShapeDtypeStruct((m, D_MODEL), F32),
        compiler_params=_params(1),
        name="ffn",
    )(x, merged, w["w_out"], w["g_ffn"], w["w_up"], w["w_down"], w["g_final"])


SAMPLE_PROJ_COLS = 2048


def _sample_proj_kernel(x_ref, g_ref, w_ref, o_ref, wb_ref):
    h = _rms(x_ref[...], g_ref[...]).astype(BF16)
    wb = w_ref[...].astype(BF16)
    for q in range(SAMPLE_PROJ_COLS // HEAD_DIM):
        wb_ref[q] = wb[:, q * HEAD_DIM:(q + 1) * HEAD_DIM]
    o_ref[...] = _dot(h, wb)


def _sample_proj(x, g_mix, w_in):
    m = x.shape[0]
    slabs = SAMPLE_PROJ_COLS // HEAD_DIM
    return pl.pallas_call(
        _sample_proj_kernel,
        grid=(D_IN // SAMPLE_PROJ_COLS,),
        in_specs=[_const_spec((m, D_MODEL)), _const_spec((1, D_MODEL)),
                  pl.BlockSpec((D_MODEL, SAMPLE_PROJ_COLS), lambda j: (0, j))],
        out_specs=[pl.BlockSpec((m, SAMPLE_PROJ_COLS), lambda j: (0, j)),
                   pl.BlockSpec((slabs, D_MODEL, HEAD_DIM), lambda j: (j, 0, 0))],
        out_shape=[jax.ShapeDtypeStruct((m, D_IN), F32),
                   jax.ShapeDtypeStruct((D_IN // HEAD_DIM, D_MODEL, HEAD_DIM), BF16)],
        compiler_params=_params(1),
        name="sample_proj",
    )(x, g_mix, w_in)


def _head_rows(ref, bi, hd):
    halves = [ref[bi, pl.ds(half * N_HEADS + hd, N_MEM, stride=CACHE_ROWS), :]
              for half in range(HEAD_DIM // LANES)]
    return jnp.concatenate(halves, axis=-1)


SAMPLE_CONV_LANES = LANES


def _sample_conv_kernel(l, a_ref, g_ref, st_ref, cw_ref, cb_ref, lng_ref, lnb_ref, wco_ref,
                        conv_ref, st_out_ref, u_scr, y_scr):
    j = pl.program_id(0)
    nb = st_ref.shape[1]
    u_scr[...] = a_ref[...] * jax.nn.sigmoid(g_ref[...])

    def plane(k):
        return st_ref[k] if k < HIST else u_scr[pl.ds(k - HIST, nb, stride=l), :]

    for w in range(HIST):
        st_out_ref[w] = plane(w + l)
    for i in range(l):
        acc = jnp.broadcast_to(cb_ref[...], (nb, SAMPLE_CONV_LANES))
        for w in range(CONV_WIDTH):
            acc = acc + plane(i + w) * cw_ref[w:w + 1, :]
        y_scr[j, pl.ds(i, nb, stride=l), :] = acc

    @pl.when(j == pl.num_programs(0) - 1)
    def _():
        y = jnp.concatenate([y_scr[n] for n in range(D_MODEL // SAMPLE_CONV_LANES)], axis=-1)
        act = _silu(_standardize(y) * lng_ref[...] + lnb_ref[...]).astype(BF16)
        for n in range(D_MODEL // HEAD_DIM):
            conv_ref[:, n * HEAD_DIM:(n + 1) * HEAD_DIM] = _dot(act, wco_ref[n])


def _sample_conv(proj, state_t, w, l):
    m = proj.shape[0]
    nb = state_t.shape[1]
    lanes = SAMPLE_CONV_LANES
    glu_a = pl.BlockSpec((m, lanes), lambda j: (0, OFF_GLU_A // lanes + j))
    glu_g = pl.BlockSpec((m, lanes), lambda j: (0, OFF_GLU_G // lanes + j))
    st = pl.BlockSpec((HIST, nb, lanes), lambda j: (0, 0, j))
    row_blk = pl.BlockSpec((1, lanes), lambda j: (0, j))
    return pl.pallas_call(
        functools.partial(_sample_conv_kernel, l),
        grid=(D_MODEL // lanes,),
        in_specs=[glu_a, glu_g, st, pl.BlockSpec((CONV_WIDTH, lanes), lambda j: (0, j)), row_blk,
                  _const_spec((1, D_MODEL)), _const_spec((1, D_MODEL)),
                  _const_spec((D_MODEL // HEAD_DIM, D_MODEL, HEAD_DIM))],
        out_specs=[pl.BlockSpec((m, D_MODEL), lambda j: (0, 0)), st],
        out_shape=[jax.ShapeDtypeStruct((m, D_MODEL), F32),
                   jax.ShapeDtypeStruct(state_t.shape, F32)],
        scratch_shapes=[pltpu.VMEM((m, lanes), F32),
                        pltpu.VMEM((D_MODEL // lanes, m, lanes), F32)],
        compiler_params=_params(1),
        name="sample_conv",
    )(proj, proj, state_t, w["conv_w"], w["conv_b"], w["conv_ln_g"], w["conv_ln_b"],
      w["w_conv_out"])


def _sample_mixer_kernel(chunk_decay, l,
                         p_ref, conv_ref, cos_ref, sin_ref, s_ref, k_ref, v_ref,
                         gn_ref, dmat_ref, qdec_ref, kdec_ref,
                         merged_ref, sout_ref):
    rows = B_SAMPLE * l
    pad = LANES
    row_id = lax.broadcasted_iota(jnp.int32, (rows, HEAD_DIM), 0)
    col_id = lax.broadcasted_iota(jnp.int32, (HEAD_DIM, pad), 1)
    row_of = [(row_id >= bi * l) & (row_id < (bi + 1) * l) for bi in range(B_SAMPLE)]
    col_of = [(col_id >= bi * l) & (col_id < (bi + 1) * l) for bi in range(B_SAMPLE)]

    def proj(off, n=HEAD_DIM):
        return p_ref[:, off:off + n]

    cos, sin = cos_ref[...], sin_ref[...]
    zeros_pad = jnp.zeros((pad - rows, HEAD_DIM), F32)

    def first_stage(hd):
        qb = _rope(proj(OFF_Q + hd * HEAD_DIM), cos, sin).astype(BF16)
        kr = _rope(proj(OFF_K + hd * HEAD_DIM), cos, sin) * QK_SCALE
        v = proj(OFF_V + hd * HEAD_DIM)
        kb_pad = jnp.concatenate([kr, zeros_pad], axis=0).astype(BF16)
        vb_pad = jnp.concatenate([v, zeros_pad], axis=0).astype(BF16)
        scores = _dot_nt(qb, kb_pad) * dmat_ref[hd]
        kd_t = jnp.concatenate([kr * kdec_ref[hd], zeros_pad], axis=0).T
        qx = proj(OFF_XA + hd * HEAD_DIM).astype(BF16)
        o_cross = jnp.zeros((rows, HEAD_DIM), F32)
        s = jnp.zeros((rows, N_MEM), F32)
        for bi in range(B_SAMPLE):
            s_old = s_ref[bi, hd]
            o_cross = jnp.where(row_of[bi], _dot(qb, s_old.astype(BF16)), o_cross)
            kd_bi = jnp.where(col_of[bi], kd_t, 0.0).astype(BF16)
            sout_ref[bi, hd] = s_old * chunk_decay[hd] + _dot(kd_bi, vb_pad)
            s = jnp.where(row_of[bi], _dot_nt(qx, _head_rows(k_ref, bi, hd).astype(BF16)), s)
        return scores, vb_pad, o_cross, s * QK_SCALE

    def second_stage(hd, scores, vb_pad, o_cross, s):
        sl = slice(hd * HEAD_DIM, (hd + 1) * HEAD_DIM)
        o = _dot(scores.astype(BF16), vb_pad) + o_cross * qdec_ref[hd]
        ret = _standardize(o) * gn_ref[:, sl] * _silu(proj(OFF_G + hd * HEAD_DIM))
        e = jnp.exp(s - jnp.max(s, axis=-1, keepdims=True))
        pb = (e / jnp.sum(e, axis=-1, keepdims=True)).astype(BF16)
        xa = jnp.zeros((rows, HEAD_DIM), F32)
        for bi in range(B_SAMPLE):
            xa = jnp.where(row_of[bi], _dot(pb, _head_rows(v_ref, bi, hd).astype(BF16)), xa)
        g0 = jax.nn.sigmoid(proj(OFF_GATE + hd * HEAD_DIM))
        g2 = jax.nn.sigmoid(proj(OFF_GATE + 2 * D_MODEL + hd * HEAD_DIM))
        return g0 * ret + g2 * xa

    parts = []
    staged = first_stage(0)
    for hd in range(N_HEADS):
        following = first_stage(hd + 1) if hd + 1 < N_HEADS else None
        parts.append(second_stage(hd, *staged))
        staged = following
    g1 = jax.nn.sigmoid(proj(OFF_GATE + D_MODEL, D_MODEL))
    merged_ref[...] = jnp.concatenate(parts, axis=-1) + g1 * conv_ref[...]


def _ffn_with_sample_mixer(x, merged, w, proj, conv, state_ret, cache_k, cache_v, l):
    m = x.shape[0]
    nb = state_ret.shape[0]
    steps = nb // B_SAMPLE
    t_rows = m // steps
    assert steps * B_SAMPLE == nb and steps * t_rows == m and t_rows % FFN_ROWS == 0
    rows = B_SAMPLE * l
    pad = LANES
    cos, sin = _rope_tables(PAST_LEN + jnp.arange(l))
    cos, sin = jnp.tile(cos, (B_SAMPLE, 1)), jnp.tile(sin, (B_SAMPLE, 1))
    intra, q_decay, k_decay, chunk_decay = _decay_tables(_log_gammas(), l)
    same_req = (jnp.arange(rows)[:, None] // l) == (jnp.arange(pad)[None, :] // l)
    dmat = jnp.where(same_req[None], jnp.pad(jnp.tile(intra, (1, B_SAMPLE, B_SAMPLE)),
                                              ((0, 0), (0, 0), (0, pad - rows))), 0.0)
    qdec = jnp.broadcast_to(jnp.tile(q_decay, (1, B_SAMPLE))[:, :, None], (N_HEADS, rows, HEAD_DIM))
    kdec = jnp.broadcast_to(jnp.tile(k_decay, (1, B_SAMPLE))[:, :, None], (N_HEADS, rows, HEAD_DIM))

    n_ffn_in = 7

    def body(cd_ref, *refs):
        cd = tuple(cd_ref[i] for i in range(N_HEADS))
        ffn_in, rest = refs[:n_ffn_in], refs[n_ffn_in:]
        mixer_in, (y_ref, merged_ref, sout_ref) = rest[:-3], rest[-3:]
        _ffn_kernel(*ffn_in, y_ref)
        _sample_mixer_kernel(cd, l, *mixer_in, merged_ref, sout_ref)

    row = _const_spec((1, D_MODEL))
    ptok = pl.BlockSpec((t_rows, D_MODEL), lambda i: (i, 0))
    state = pl.BlockSpec((B_SAMPLE, N_HEADS, HEAD_DIM, HEAD_DIM), lambda i: (i, 0, 0, 0))
    tok = pl.BlockSpec((rows, D_MODEL), lambda i: (i, 0))
    cache = pl.BlockSpec((B_SAMPLE, N_MEM * CACHE_ROWS, LANES), lambda i: (i, 0, 0))
    return pl.pallas_call(
        body,
        grid=(steps,),
        in_specs=[pl.BlockSpec(memory_space=pltpu.SMEM),
                  ptok, ptok, _const_spec((D_MODEL, D_MODEL)), row, _const_spec((D_MODEL, D_FF)),
                  _const_spec((D_FF, D_MODEL)), row,
                  pl.BlockSpec((rows, D_IN), lambda i: (i, 0)), tok,
                  _const_spec((rows, HALF)), _const_spec((rows, HALF)),
                  state, cache, cache, row,
                  _const_spec((N_HEADS, rows, pad)),
                  _const_spec((N_HEADS, rows, HEAD_DIM)), _const_spec((N_HEADS, rows, HEAD_DIM))],
        out_specs=[ptok, tok, state],
        out_shape=[jax.ShapeDtypeStruct((m, D_MODEL), F32),
                   jax.ShapeDtypeStruct((nb * l, D_MODEL), F32),
                   jax.ShapeDtypeStruct(state_ret.shape, F32)],
        compiler_params=_params(1),
        name="ffn_sample_mixer",
    )(chunk_decay, x, merged, w["w_out"], w["g_ffn"], w["w_up"], w["w_down"], w["g_final"],
      proj, conv, cos, sin, state_ret, cache_k, cache_v, w["ret_gn_g"], dmat, qdec, kdec)


def kernel(x_prompt, x_sample, mem_prompt, state_ret, state_conv, cache_mem_k, cache_mem_v, g_mix, w_in, ret_gn_g, conv_w, conv_b, conv_ln_g, conv_ln_b, w_conv_out, w_out, g_ffn, w_up, w_down, g_mem, w_mem_kv, g_final):
    assert state_ret.shape[0] == 1, "one layer"
    b_p, l_p, _ = x_prompt.shape
    b_s, l_s, _ = x_sample.shape
    w = {
        "g_mix": g_mix[0][None], "ret_gn_g": ret_gn_g[0][None],
        "conv_w": conv_w[0], "conv_b": conv_b[0][None], "conv_ln_g": conv_ln_g[0][None],
        "conv_ln_b": conv_ln_b[0][None], "w_conv_out": _column_slabs(w_conv_out[0].astype(BF16)),
        "g_ffn": g_ffn[0][None], "g_final": g_final[None],
    }

    xs = x_sample.reshape(b_s * l_s, D_MODEL)
    proj_s, w["w_in"] = _sample_proj(xs, w["g_mix"], w_in[0])
    conv_out_s, conv_s = _sample_conv(proj_s, jnp.transpose(state_conv[0], (1, 0, 2)), w, l_s)
    conv_s = jnp.transpose(conv_s, (1, 0, 2))

    mk, mv, mkb, mvb = _memkv(mem_prompt, g_mem[0][None], w_mem_kv[0])
    merged_p, ret_p, conv_p, w["w_out"], w["w_up"], w["w_down"] = _prompt_mixer(
        x_prompt, mkb, mvb, w, (w_out[0], w_up[0], w_down[0]))
    y_prompt, merged_s, ret_s = _ffn_with_sample_mixer(
        x_prompt.reshape(b_p * l_p, D_MODEL), merged_p.reshape(b_p * l_p, D_MODEL), w,
        proj_s, conv_out_s, state_ret[0],
        _cache_rows(cache_mem_k[0]), _cache_rows(cache_mem_v[0]), l_s)
    y_sample = _ffn(xs, merged_s, w)

    return (y_prompt.reshape(b_p, l_p, D_MODEL), y_sample.reshape(b_s, l_s, D_MODEL),
            ret_p[None], conv_p[None], mk[None], mv[None], ret_s[None], conv_s[None])
```

```python
import functools

import jax
import jax.numpy as jnp
from jax import lax
from jax.experimental import pallas as pl
from jax.experimental.pallas import tpu as pltpu

F32 = jnp.float32
BF16 = jnp.bfloat16

D_MODEL = 1024
N_HEADS = 4
HEAD_DIM = 256
HALF = HEAD_DIM // 2
D_FF = 4 * D_MODEL
CONV_WIDTH = 31
HIST = CONV_WIDTH - 1
N_MEM = 256
PAST_LEN = 16384
ROPE_BASE = 10000.0
EPS = 1e-6
QK_SCALE = HEAD_DIM ** -0.5

OFF_Q, OFF_K, OFF_V, OFF_G = 0, 1024, 2048, 3072
OFF_GLU_A, OFF_GLU_G, OFF_XA, OFF_GATE = 4096, 5120, 6144, 7168
D_IN = 10240

SUBLANES = 8
HIST_PAD = 32
T_PROMPT = 256
CHUNKS_PER_STEP = 2
T_FFN = 512
FFN_ROWS = 256
B_SAMPLE = 2
VMEM_LIMIT = 56 * 1024 * 1024

LANES = 128
CACHE_ROWS = N_HEADS * HEAD_DIM // LANES

NT_DIMS = (((1,), (1,)), ((), ()))


def _cache_rows(c):
    b = c.shape[0]
    c = c.reshape(b, N_MEM, N_HEADS, HEAD_DIM // LANES, LANES).transpose(0, 1, 3, 2, 4)
    return c.reshape(b, N_MEM * CACHE_ROWS, LANES)


def _column_slabs(w):
    k, n = w.shape
    return w.reshape(k, n // HEAD_DIM, HEAD_DIM).transpose(1, 0, 2)


def _cache_from_rows(r):
    b = r.shape[0]
    r = r.reshape(b, N_MEM, HEAD_DIM // LANES, N_HEADS, LANES).transpose(0, 1, 3, 2, 4)
    return r.reshape(b, N_MEM, N_HEADS, HEAD_DIM)


def _dot(a, b):
    return jnp.dot(a, b, preferred_element_type=F32)


def _dot_nt(a, b):
    return lax.dot_general(a, b, NT_DIMS, preferred_element_type=F32)


def _rms(x, g):
    return x * lax.rsqrt(jnp.mean(x * x, axis=-1, keepdims=True) + EPS) * g


def _standardize(x):
    mu = jnp.mean(x, axis=-1, keepdims=True)
    xc = x - mu
    return xc * lax.rsqrt(jnp.mean(xc * xc, axis=-1, keepdims=True) + EPS)


def _silu(x):
    return x * jax.nn.sigmoid(x)


def _rope(x, cos, sin):
    x1, x2 = x[:, :HALF], x[:, HALF:]
    return jnp.concatenate([x1 * cos - x2 * sin, x2 * cos + x1 * sin], axis=-1)


def _const_spec(shape):
    return pl.BlockSpec(shape, lambda *_: (0,) * len(shape), pipeline_mode=pl.Buffered(1))


def _params(n_grid):
    return pltpu.CompilerParams(dimension_semantics=("arbitrary",) * n_grid,
                                vmem_limit_bytes=VMEM_LIMIT)


def _memkv_kernel(mem_ref, g_ref, w_ref, k_ref, v_ref, kb_ref, vb_ref, wb_scr):
    @pl.when(pl.program_id(0) == 0)
    def _():
        wb_scr[...] = w_ref[...].astype(BF16)

    h = _rms(mem_ref[0], g_ref[...]).astype(BF16)
    kv = _dot(h, wb_scr[...])
    k, v = kv[:, :D_MODEL], kv[:, D_MODEL:]
    for hd in range(N_HEADS):
        kb_ref[0, hd] = k[:, hd * HEAD_DIM:(hd + 1) * HEAD_DIM].astype(BF16)
        vb_ref[0, hd] = v[:, hd * HEAD_DIM:(hd + 1) * HEAD_DIM].astype(BF16)
    for hd in range(N_HEADS):
        for half in range(HEAD_DIM // LANES):
            rows = pl.ds(half * N_HEADS + hd, N_MEM, stride=CACHE_ROWS)
            cols = slice(hd * HEAD_DIM + half * LANES, hd * HEAD_DIM + (half + 1) * LANES)
            k_ref[0, rows, :] = k[:, cols]
            v_ref[0, rows, :] = v[:, cols]


def _memkv(mem, g_mem, w_mem_kv):
    b = mem.shape[0]
    blk = pl.BlockSpec((1, N_MEM, D_MODEL), lambda i: (i, 0, 0))
    rows = pl.BlockSpec((1, N_MEM * CACHE_ROWS, LANES), lambda i: (i, 0, 0))
    slabs = pl.BlockSpec((1, N_HEADS, N_MEM, HEAD_DIM), lambda i: (i, 0, 0, 0))
    k, v, kb, vb = pl.pallas_call(
        _memkv_kernel,
        grid=(b,),
        in_specs=[blk, _const_spec((1, D_MODEL)), _const_spec((D_MODEL, 2 * D_MODEL))],
        out_specs=[rows, rows, slabs, slabs],
        out_shape=[jax.ShapeDtypeStruct((b, N_MEM * CACHE_ROWS, LANES), F32)] * 2
        + [jax.ShapeDtypeStruct((b, N_HEADS, N_MEM, HEAD_DIM), BF16)] * 2,
        scratch_shapes=[pltpu.VMEM((D_MODEL, 2 * D_MODEL), BF16)],
        compiler_params=_params(1),
        name="memkv",
    )(mem, g_mem, w_mem_kv)
    return _cache_from_rows(k), _cache_from_rows(v), kb, vb


CONV_ROWS = 128


def _depthwise_conv(ext_ref, cw_ref, cb_ref, y_ref, row0, t, lane_slices):
    first = HIST_PAD - HIST
    for j in lane_slices:
        cols = slice(j * LANES, (j + 1) * LANES)
        for r0 in range(row0, row0 + t, CONV_ROWS):
            acc = jnp.broadcast_to(cb_ref[:, cols], (CONV_ROWS, LANES))
            for r in range(SUBLANES):
                rows = CONV_ROWS + (SUBLANES if r else 0)
                part = None
                for off in range(r, first + CONV_WIDTH, SUBLANES):
                    if off < first:
                        continue
                    w = off - first
                    base = r0 + off - r
                    term = ext_ref[base:base + rows, cols] * cw_ref[w:w + 1, cols]
                    part = term if part is None else part + term
                acc = acc + part[r:r + CONV_ROWS]
            y_ref[r0:r0 + CONV_ROWS, cols] = acc


def _prompt_mixer_kernel(chunk_decay,
                         x_ref, cos_ref, sin_ref, gmix_ref, win_ref, gn_ref, cw_ref, cb_ref,
                         lng_ref, lnb_ref, wco_ref, kb_ref, vb_ref, dmat_ref, qdec_ref, kdec_ref,
                         wout_f_ref, wup_f_ref, wdown_f_ref,
                         merged_ref, s_ref, cout_ref, wout_b_ref, wup_b_ref, wdown_b_ref,
                         ext_scr, y_scr, conv_scr, part_scr, g1_scr):
    t = T_PROMPT
    rows_step = CHUNKS_PER_STEP * t

    for f_ref, b_ref in ((wout_f_ref, wout_b_ref), (wup_f_ref, wup_b_ref),
                         (wdown_f_ref, wdown_b_ref)):
        b_ref[...] = f_ref[...].astype(BF16)

    @pl.when(pl.program_id(1) == 0)
    def _():
        s_ref[...] = jnp.zeros_like(s_ref)
        ext_scr[0:HIST_PAD, :] = jnp.zeros((HIST_PAD, D_MODEL), F32)

    h_all = [_rms(x_ref[0, c * t:(c + 1) * t, :], gmix_ref[...]).astype(BF16)
             for c in range(CHUNKS_PER_STEP)]

    def glu(c, j):
        a = _dot(h_all[c], win_ref[(OFF_GLU_A + j * HEAD_DIM) // HEAD_DIM])
        g = _dot(h_all[c], win_ref[(OFF_GLU_G + j * HEAD_DIM) // HEAD_DIM])
        ext_scr[HIST_PAD + c * t:HIST_PAD + (c + 1) * t, j * HEAD_DIM:(j + 1) * HEAD_DIM] = (
            a * jax.nn.sigmoid(g))

    for j in range(D_MODEL // HEAD_DIM):
        glu(0, j)

    conv_slices = D_MODEL // LANES // N_HEADS
    last = N_HEADS - 1
    for c in range(CHUNKS_PER_STEP):
        rows = slice(c * t, (c + 1) * t)
        h = h_all[c]
        cos, sin = cos_ref[rows, :], sin_ref[rows, :]

        def proj(off, h=h):
            return _dot(h, win_ref[off // HEAD_DIM])

        for hd in range(N_HEADS):
            sl = slice(hd * HEAD_DIM, (hd + 1) * HEAD_DIM)
            if c + 1 < CHUNKS_PER_STEP:
                glu(c + 1, hd)
            _depthwise_conv(ext_scr, cw_ref, cb_ref, y_scr, c * t, t,
                            range(hd * conv_slices, (hd + 1) * conv_slices))
            qb = _rope(proj(OFF_Q + hd * HEAD_DIM), cos, sin).astype(BF16)
            kr = _rope(proj(OFF_K + hd * HEAD_DIM), cos, sin) * QK_SCALE
            vb = proj(OFF_V + hd * HEAD_DIM).astype(BF16)
            swish_g = _silu(proj(OFF_G + hd * HEAD_DIM))
            scores = _dot_nt(qb, kr.astype(BF16)) * dmat_ref[hd]
            qx = proj(OFF_XA + hd * HEAD_DIM).astype(BF16)
            s_old = s_ref[0, hd]
            o_cross = _dot(qb, s_old.astype(BF16)) * qdec_ref[hd]
            g0 = jax.nn.sigmoid(proj(OFF_GATE + hd * HEAD_DIM))
            o = _dot(scores.astype(BF16), vb) + o_cross
            s = _dot_nt(qx, kb_ref[0, hd]) * QK_SCALE
            g2 = jax.nn.sigmoid(proj(OFF_GATE + 2 * D_MODEL + hd * HEAD_DIM))
            kd_t = (kr * kdec_ref[hd]).T.astype(BF16)
            s_ref[0, hd] = s_old * chunk_decay[hd] + _dot(kd_t, vb)
            g1 = jax.nn.sigmoid(proj(OFF_GATE + D_MODEL + hd * HEAD_DIM))
            if hd == last:
                act = _silu(_standardize(y_scr[rows, :]) * lng_ref[...] + lnb_ref[...])
                act = act.astype(BF16)
                for n in range(D_MODEL // HEAD_DIM):
                    conv_scr[rows, n * HEAD_DIM:(n + 1) * HEAD_DIM] = _dot(act, wco_ref[n])
                done = slice(0, hd * HEAD_DIM)
                merged_ref[0, rows, done] = (
                    part_scr[rows, done] + g1_scr[rows, done] * conv_scr[rows, done]).astype(BF16)
            e = jnp.exp(s - jnp.max(s, axis=-1, keepdims=True))
            p = e / jnp.sum(e, axis=-1, keepdims=True)
            xa = _dot(p.astype(BF16), vb_ref[0, hd])
            ret = _standardize(o) * gn_ref[:, sl] * swish_g
            part = g0 * ret + g2 * xa
            if hd < last:
                part_scr[rows, sl] = part
                g1_scr[rows, sl] = g1
            else:
                merged_ref[0, rows, sl] = (part + g1 * conv_scr[rows, sl]).astype(BF16)

    cout_ref[0] = ext_scr[rows_step + HIST_PAD - HIST:rows_step + HIST_PAD, :]
    ext_scr[0:HIST_PAD, :] = ext_scr[rows_step:rows_step + HIST_PAD, :]


def _decay_tables(log_gamma, c):
    idx = jnp.arange(c, dtype=F32)
    diff = idx[:, None] - idx[None, :]
    causal = diff >= 0
    intra = jnp.where(causal[None],
                      jnp.exp(log_gamma[:, None, None] * jnp.where(causal, diff, 0.0)[None]), 0.0)
    q_decay = jnp.exp(log_gamma[:, None] * (idx + 1.0))
    k_decay = jnp.exp(log_gamma[:, None] * (c - 1.0 - idx))
    chunk_decay = jnp.exp(log_gamma * c)
    return intra, q_decay, k_decay, chunk_decay


def _log_gammas():
    return jnp.log1p(-jnp.exp2(-5.0 - jnp.arange(N_HEADS, dtype=F32)))


def _rope_tables(pos):
    inv = ROPE_BASE ** (-jnp.arange(0, HEAD_DIM, 2, dtype=F32) / HEAD_DIM)
    ang = pos.astype(F32)[:, None] * inv[None, :]
    return jnp.cos(ang), jnp.sin(ang)


def _prompt_mixer(x, kb, vb, w, mlp_weights):
    b, l, _ = x.shape
    t = T_PROMPT
    rows_step = CHUNKS_PER_STEP * t
    n_chunks = l // rows_step
    steps = b * n_chunks

    def row_block(wt):
        assert wt.shape[0] % (steps * 2 * SUBLANES) == 0
        return pl.BlockSpec((wt.shape[0] // steps, wt.shape[1]), lambda i, j: (i * n_chunks + j, 0))

    mlp_blocks = [row_block(wt) for wt in mlp_weights]
    cos, sin = _rope_tables(jnp.arange(l))
    intra, q_decay, k_decay, chunk_decay = _decay_tables(_log_gammas(), t)
    qdec = jnp.broadcast_to(q_decay[:, :, None], (N_HEADS, t, HEAD_DIM))
    kdec = jnp.broadcast_to(k_decay[:, :, None], (N_HEADS, t, HEAD_DIM))

    def body(cd_ref, *refs):
        cd = tuple(cd_ref[i] for i in range(N_HEADS))
        _prompt_mixer_kernel(cd, *refs)

    tok = pl.BlockSpec((1, rows_step, D_MODEL), lambda i, j: (i, j, 0))
    rope = pl.BlockSpec((rows_step, HALF), lambda i, j: (j, 0))
    mem = pl.BlockSpec((1, N_HEADS, N_MEM, HEAD_DIM), lambda i, j: (i, 0, 0, 0))
    row = _const_spec((1, D_MODEL))
    table = _const_spec((N_HEADS, t, HEAD_DIM))
    return pl.pallas_call(
        body,
        grid=(b, n_chunks),
        in_specs=[pl.BlockSpec(memory_space=pltpu.SMEM),
                  tok, rope, rope, row, _const_spec((D_IN // HEAD_DIM, D_MODEL, HEAD_DIM)), row,
                  _const_spec((CONV_WIDTH, D_MODEL)), row, row, row,
                  _const_spec((D_MODEL // HEAD_DIM, D_MODEL, HEAD_DIM)), mem, mem,
                  table, table, table] + mlp_blocks,
        out_specs=[tok,
                   pl.BlockSpec((1, N_HEADS, HEAD_DIM, HEAD_DIM), lambda i, j: (i, 0, 0, 0)),
                   pl.BlockSpec((1, HIST, D_MODEL), lambda i, j: (i, 0, 0))] + mlp_blocks,
        out_shape=[jax.ShapeDtypeStruct((b, l, D_MODEL), BF16),
                   jax.ShapeDtypeStruct((b, N_HEADS, HEAD_DIM, HEAD_DIM), F32),
                   jax.ShapeDtypeStruct((b, HIST, D_MODEL), F32)]
        + [jax.ShapeDtypeStruct(wt.shape, BF16) for wt in mlp_weights],
        scratch_shapes=[pltpu.VMEM((HIST_PAD + rows_step, D_MODEL), F32)]
        + [pltpu.VMEM((rows_step, D_MODEL), F32)] * 4,
        compiler_params=_params(2),
        name="prompt_mixer",
    )(chunk_decay, x, cos, sin, w["g_mix"], w["w_in"], w["ret_gn_g"], w["conv_w"],
      w["conv_b"], w["conv_ln_g"], w["conv_ln_b"], w["w_conv_out"], kb, vb, intra, qdec, kdec,
      *mlp_weights)


def _ffn_kernel(x_ref, m_ref, wout_ref, gffn_ref, wup_ref, wdown_ref, gfin_ref, y_ref):
    groups = [slice(r, r + FFN_ROWS) for r in range(0, x_ref.shape[0], FFN_ROWS)]
    x1 = [x_ref[g, :] + _dot(m_ref[g, :].astype(BF16), wout_ref[...]) for g in groups]
    h2 = [_rms(v, gffn_ref[...]).astype(BF16) for v in x1]
    a = [jnp.square(jnp.maximum(_dot(v, wup_ref[...]), 0.0)).astype(BF16) for v in h2]
    for g, v, act in zip(groups, x1, a):
        y_ref[g, :] = _rms(v + _dot(act, wdown_ref[...]), gfin_ref[...])


def _ffn(x, merged, w):
    m = x.shape[0]
    tok = pl.BlockSpec((T_FFN, D_MODEL), lambda i: (i, 0))
    row = _const_spec((1, D_MODEL))
    return pl.pallas_call(
        _ffn_kernel,
        grid=(m // T_FFN,),
        in_specs=[tok, tok, _const_spec((D_MODEL, D_MODEL)), row, _const_spec((D_MODEL, D_FF)),
                  _const_spec((D_FF, D_MODEL)), row],
        out_specs=tok,
        out_shape=jax.ShapeDtypeStruct((m, D_MODEL), F32),
        compiler_params=_params(1),
        name="ffn",
    )(x, merged, w["w_out"], w["g_ffn"], w["w_up"], w["w_down"], w["g_final"])


SAMPLE_PROJ_COLS = 2048


def _sample_proj_kernel(x_ref, g_ref, w_ref, o_ref, wb_ref):
    h = _rms(x_ref[...], g_ref[...]).astype(BF16)
    wb = w_ref[...].astype(BF16)
    for q in range(SAMPLE_PROJ_COLS // HEAD_DIM):
        wb_ref[q] = wb[:, q * HEAD_DIM:(q + 1) * HEAD_DIM]
    o_ref[...] = _dot(h, wb)


def _sample_proj(x, g_mix, w_in):
    m = x.shape[0]
    slabs = SAMPLE_PROJ_COLS // HEAD_DIM
    return pl.pallas_call(
        _sample_proj_kernel,
        grid=(D_IN // SAMPLE_PROJ_COLS,),
        in_specs=[_const_spec((m, D_MODEL)), _const_spec((1, D_MODEL)),
                  pl.BlockSpec((D_MODEL, SAMPLE_PROJ_COLS), lambda j: (0, j))],
        out_specs=[pl.BlockSpec((m, SAMPLE_PROJ_COLS), lambda j: (0, j)),
                   pl.BlockSpec((slabs, D_MODEL, HEAD_DIM), lambda j: (j, 0, 0))],
        out_shape=[jax.ShapeDtypeStruct((m, D_IN), F32),
                   jax.ShapeDtypeStruct((D_IN // HEAD_DIM, D_MODEL, HEAD_DIM), BF16)],
        compiler_params=_params(1),
        name="sample_proj",
    )(x, g_mix, w_in)


def _head_rows(ref, bi, hd):
    halves = [ref[bi, pl.ds(half * N_HEADS + hd, N_MEM, stride=CACHE_ROWS), :]
              for half in range(HEAD_DIM // LANES)]
    return jnp.concatenate(halves, axis=-1)


SAMPLE_CONV_LANES = LANES


def _sample_conv_kernel(l, a_ref, g_ref, st_ref, cw_ref, cb_ref, lng_ref, lnb_ref, wco_ref,
                        conv_ref, st_out_ref, u_scr, y_scr):
    j = pl.program_id(0)
    nb = st_ref.shape[1]
    u_scr[...] = a_ref[...] * jax.nn.sigmoid(g_ref[...])

    def plane(k):
        return st_ref[k] if k < HIST else u_scr[pl.ds(k - HIST, nb, stride=l), :]

    for w in range(HIST):
        st_out_ref[w] = plane(w + l)
    for i in range(l):
        acc = jnp.broadcast_to(cb_ref[...], (nb, SAMPLE_CONV_LANES))
        for w in range(CONV_WIDTH):
            acc = acc + plane(i + w) * cw_ref[w:w + 1, :]
        y_scr[j, pl.ds(i, nb, stride=l), :] = acc

    @pl.when(j == pl.num_programs(0) - 1)
    def _():
        y = jnp.concatenate([y_scr[n] for n in range(D_MODEL // SAMPLE_CONV_LANES)], axis=-1)
        act = _silu(_standardize(y) * lng_ref[...] + lnb_ref[...]).astype(BF16)
        for n in range(D_MODEL // HEAD_DIM):
            conv_ref[:, n * HEAD_DIM:(n + 1) * HEAD_DIM] = _dot(act, wco_ref[n])


def _sample_conv(proj, state_t, w, l):
    m = proj.shape[0]
    nb = state_t.shape[1]
    lanes = SAMPLE_CONV_LANES
    glu_a = pl.BlockSpec((m, lanes), lambda j: (0, OFF_GLU_A // lanes + j))
    glu_g = pl.BlockSpec((m, lanes), lambda j: (0, OFF_GLU_G // lanes + j))
    st = pl.BlockSpec((HIST, nb, lanes), lambda j: (0, 0, j))
    row_blk = pl.BlockSpec((1, lanes), lambda j: (0, j))
    return pl.pallas_call(
        functools.partial(_sample_conv_kernel, l),
        grid=(D_MODEL // lanes,),
        in_specs=[glu_a, glu_g, st, pl.BlockSpec((CONV_WIDTH, lanes), lambda j: (0, j)), row_blk,
                  _const_spec((1, D_MODEL)), _const_spec((1, D_MODEL)),
                  _const_spec((D_MODEL // HEAD_DIM, D_MODEL, HEAD_DIM))],
        out_specs=[pl.BlockSpec((m, D_MODEL), lambda j: (0, 0)), st],
        out_shape=[jax.ShapeDtypeStruct((m, D_MODEL), F32),
                   jax.ShapeDtypeStruct(state_t.shape, F32)],
        scratch_shapes=[pltpu.VMEM((m, lanes), F32),
                        pltpu.VMEM((D_MODEL // lanes, m, lanes), F32)],
        compiler_params=_params(1),
        name="sample_conv",
    )(proj, proj, state_t, w["conv_w"], w["conv_b"], w["conv_ln_g"], w["conv_ln_b"],
      w["w_conv_out"])


def _sample_mixer_kernel(chunk_decay, l,
                         p_ref, conv_ref, cos_ref, sin_ref, s_ref, k_ref, v_ref,
                         gn_ref, dmat_ref, qdec_ref, kdec_ref,
                         merged_ref, sout_ref):
    rows = B_SAMPLE * l
    pad = LANES
    row_id = lax.broadcasted_iota(jnp.int32, (rows, HEAD_DIM), 0)
    col_id = lax.broadcasted_iota(jnp.int32, (HEAD_DIM, pad), 1)
    row_of = [(row_id >= bi * l) & (row_id < (bi + 1) * l) for bi in range(B_SAMPLE)]
    col_of = [(col_id >= bi * l) & (col_id < (bi + 1) * l) for bi in range(B_SAMPLE)]

    def proj(off, n=HEAD_DIM):
        return p_ref[:, off:off + n]

    cos, sin = cos_ref[...], sin_ref[...]
    zeros_pad = jnp.zeros((pad - rows, HEAD_DIM), F32)

    def first_stage(hd):
        qb = _rope(proj(OFF_Q + hd * HEAD_DIM), cos, sin).astype(BF16)
        kr = _rope(proj(OFF_K + hd * HEAD_DIM), cos, sin) * QK_SCALE
        v = proj(OFF_V + hd * HEAD_DIM)
        kb_pad = jnp.concatenate([kr, zeros_pad], axis=0).astype(BF16)
        vb_pad = jnp.concatenate([v, zeros_pad], axis=0).astype(BF16)
        scores = _dot_nt(qb, kb_pad) * dmat_ref[hd]
        kd_t = jnp.concatenate([kr * kdec_ref[hd], zeros_pad], axis=0).T
        qx = proj(OFF_XA + hd * HEAD_DIM).astype(BF16)
        o_cross = jnp.zeros((rows, HEAD_DIM), F32)
        s = jnp.zeros((rows, N_MEM), F32)
        for bi in range(B_SAMPLE):
            s_old = s_ref[bi, hd]
            o_cross = jnp.where(row_of[bi], _dot(qb, s_old.astype(BF16)), o_cross)
            kd_bi = jnp.where(col_of[bi], kd_t, 0.0).astype(BF16)
            sout_ref[bi, hd] = s_old * chunk_decay[hd] + _dot(kd_bi, vb_pad)
            s = jnp.where(row_of[bi], _dot_nt(qx, _head_rows(k_ref, bi, hd).astype(BF16)), s)
        return scores, vb_pad, o_cross, s * QK_SCALE

    def second_stage(hd, scores, vb_pad, o_cross, s):
        sl = slice(hd * HEAD_DIM, (hd + 1) * HEAD_DIM)
        o = _dot(scores.astype(BF16), vb_pad) + o_cross * qdec_ref[hd]
        ret = _standardize(o) * gn_ref[:, sl] * _silu(proj(OFF_G + hd * HEAD_DIM))
        e = jnp.exp(s - jnp.max(s, axis=-1, keepdims=True))
        pb = (e / jnp.sum(e, axis=-1, keepdims=True)).astype(BF16)
        xa = jnp.zeros((rows, HEAD_DIM), F32)
        for bi in range(B_SAMPLE):
            xa = jnp.where(row_of[bi], _dot(pb, _head_rows(v_ref, bi, hd).astype(BF16)), xa)
        g0 = jax.nn.sigmoid(proj(OFF_GATE + hd * HEAD_DIM))
        g2 = jax.nn.sigmoid(proj(OFF_GATE + 2 * D_MODEL + hd * HEAD_DIM))
        return g0 * ret + g2 * xa

    parts = []
    staged = first_stage(0)
    for hd in range(N_HEADS):
        following = first_stage(hd + 1) if hd + 1 < N_HEADS else None
        parts.append(second_stage(hd, *staged))
        staged = following
    g1 = jax.nn.sigmoid(proj(OFF_GATE + D_MODEL, D_MODEL))
    merged_ref[...] = jnp.concatenate(parts, axis=-1) + g1 * conv_ref[...]


def _ffn_with_sample_mixer(x, merged, w, proj, conv, state_ret, cache_k, cache_v, l):
    m = x.shape[0]
    nb = state_ret.shape[0]
    steps = nb // B_SAMPLE
    t_rows = m // steps
    assert steps * B_SAMPLE == nb and steps * t_rows == m and t_rows % FFN_ROWS == 0
    rows = B_SAMPLE * l
    pad = LANES
    cos, sin = _rope_tables(PAST_LEN + jnp.arange(l))
    cos, sin = jnp.tile(cos, (B_SAMPLE, 1)), jnp.tile(sin, (B_SAMPLE, 1))
    intra, q_decay, k_decay, chunk_decay = _decay_tables(_log_gammas(), l)
    same_req = (jnp.arange(rows)[:, None] // l) == (jnp.arange(pad)[None, :] // l)
    dmat = jnp.where(same_req[None], jnp.pad(jnp.tile(intra, (1, B_SAMPLE, B_SAMPLE)),
                                              ((0, 0), (0, 0), (0, pad - rows))), 0.0)
    qdec = jnp.broadcast_to(jnp.tile(q_decay, (1, B_SAMPLE))[:, :, None], (N_HEADS, rows, HEAD_DIM))
    kdec = jnp.broadcast_to(jnp.tile(k_decay, (1, B_SAMPLE))[:, :, None], (N_HEADS, rows, HEAD_DIM))

    n_ffn_in = 7

    def body(cd_ref, *refs):
        cd = tuple(cd_ref[i] for i in range(N_HEADS))
        ffn_in, rest = refs[:n_ffn_in], refs[n_ffn_in:]
        mixer_in, (y_ref, merged_ref, sout_ref) = rest[:-3], rest[-3:]
        _ffn_kernel(*ffn_in, y_ref)
        _sample_mixer_kernel(cd, l, *mixer_in, merged_ref, sout_ref)

    row = _const_spec((1, D_MODEL))
    ptok = pl.BlockSpec((t_rows, D_MODEL), lambda i: (i, 0))
    state = pl.BlockSpec((B_SAMPLE, N_HEADS, HEAD_DIM, HEAD_DIM), lambda i: (i, 0, 0, 0))
    tok = pl.BlockSpec((rows, D_MODEL), lambda i: (i, 0))
    cache = pl.BlockSpec((B_SAMPLE, N_MEM * CACHE_ROWS, LANES), lambda i: (i, 0, 0))
    return pl.pallas_call(
        body,
        grid=(steps,),
        in_specs=[pl.BlockSpec(memory_space=pltpu.SMEM),
                  ptok, ptok, _const_spec((D_MODEL, D_MODEL)), row, _const_spec((D_MODEL, D_FF)),
                  _const_spec((D_FF, D_MODEL)), row,
                  pl.BlockSpec((rows, D_IN), lambda i: (i, 0)), tok,
                  _const_spec((rows, HALF)), _const_spec((rows, HALF)),
                  state, cache, cache, row,
                  _const_spec((N_HEADS, rows, pad)),
                  _const_spec((N_HEADS, rows, HEAD_DIM)), _const_spec((N_HEADS, rows, HEAD_DIM))],
        out_specs=[ptok, tok, state],
        out_shape=[jax.ShapeDtypeStruct((m, D_MODEL), F32),
                   jax.ShapeDtypeStruct((nb * l, D_MODEL), F32),
                   jax.ShapeDtypeStruct(state_ret.shape, F32)],
        compiler_params=_params(1),
        name="ffn_sample_mixer",
    )(chunk_decay, x, merged, w["w_out"], w["g_ffn"], w["w_up"], w["w_down"], w["g_final"],
      proj, conv, cos, sin, state_ret, cache_k, cache_v, w["ret_gn_g"], dmat, qdec, kdec)


def kernel(x_prompt, x_sample, mem_prompt, state_ret, state_conv, cache_mem_k, cache_mem_v, g_mix, w_in, ret_gn_g, conv_w, conv_b, conv_ln_g, conv_ln_b, w_conv_out, w_out, g_ffn, w_up, w_down, g_mem, w_mem_kv, g_final):
    assert state_ret.shape[0] == 1, "one layer"
    b_p, l_p, _ = x_prompt.shape
    b_s, l_s, _ = x_sample.shape
    w = {
        "g_mix": g_mix[0][None], "ret_gn_g": ret_gn_g[0][None],
        "conv_w": conv_w[0], "conv_b": conv_b[0][None], "conv_ln_g": conv_ln_g[0][None],
        "conv_ln_b": conv_ln_b[0][None], "w_conv_out": _column_slabs(w_conv_out[0].astype(BF16)),
        "g_ffn": g_ffn[0][None], "g_final": g_final[None],
    }

    xs = x_sample.reshape(b_s * l_s, D_MODEL)
    proj_s, w["w_in"] = _sample_proj(xs, w["g_mix"], w_in[0])
    conv_out_s, conv_s = _sample_conv(proj_s, jnp.transpose(state_conv[0], (1, 0, 2)), w, l_s)
    conv_s = jnp.transpose(conv_s, (1, 0, 2))

    mk, mv, mkb, mvb = _memkv(mem_prompt, g_mem[0][None], w_mem_kv[0])
    merged_p, ret_p, conv_p, w["w_out"], w["w_up"], w["w_down"] = _prompt_mixer(
        x_prompt, mkb, mvb, w, (w_out[0], w_up[0], w_down[0]))
    y_prompt, merged_s, ret_s = _ffn_with_sample_mixer(
        x_prompt.reshape(b_p * l_p, D_MODEL), merged_p.reshape(b_p * l_p, D_MODEL), w,
        proj_s, conv_out_s, state_ret[0],
        _cache_rows(cache_mem_k[0]), _cache_rows(cache_mem_v[0]), l_s)
    y_sample = _ffn(xs, merged_s, w)

    return (y_prompt.reshape(b_p, l_p, D_MODEL), y_sample.reshape(b_s, l_s, D_MODEL),
            ret_p[None], conv_p[None], mk[None], mv[None], ret_s[None], conv_s[None])
```
